```python
import math
import jax, jax.numpy as jnp
from jax import lax
import numpy as np

D_MODEL = 4096
BATCH = 2
SEQ = 4096
DEPTH = 4

ATTN_WIDTH = D_MODEL // 2
SSM_WIDTH = D_MODEL - ATTN_WIDTH
MIX_WIDTH = ATTN_WIDTH + SSM_WIDTH
DIFF_HEAD_DIM = 64
DIFF_V_DIM = 2 * DIFF_HEAD_DIM
N_HEADS = ATTN_WIDTH // DIFF_V_DIM
SSM_GROUP = 16
N_SSM_GROUPS = SSM_WIDTH // SSM_GROUP
SSM_STATE = 64
IN_WIDTH = 3 * ATTN_WIDTH + SSM_WIDTH
D_FF = 4 * D_MODEL
NUM_BUCKETS = 32
MAX_EXACT = 16
MAX_DISTANCE = 128
Q_BLOCK = 128
EPS = 1e-6
DT_MIN = 1e-3
DT_MAX = 1e-1

kernel_name = "hymba_diffattn_s5_hybrid"


def rmsnorm(x, g):
    xf = x.astype(jnp.float32)
    xf = xf * lax.rsqrt(jnp.mean(xf * xf, axis=-1, keepdims=True) + EPS)
    return (xf * g.astype(jnp.float32)).astype(x.dtype)


def t5_bucket(rel):
    n = jnp.maximum(rel, 0)
    is_small = n < MAX_EXACT
    nf = jnp.maximum(n, 1).astype(jnp.float32)
    large = MAX_EXACT + (jnp.log(nf / MAX_EXACT) / math.log(MAX_DISTANCE / MAX_EXACT)
                         * (NUM_BUCKETS - MAX_EXACT)).astype(jnp.int32)
    large = jnp.minimum(large, NUM_BUCKETS - 1)
    return jnp.where(is_small, n, large)


def diff_attention(qkv, rel_bias, lq1, lk1, lq2, lk2, gain, lam_init):
    B, S, _ = qkv.shape
    q = qkv[..., :ATTN_WIDTH].reshape(B, S, N_HEADS, 2, DIFF_HEAD_DIM)
    k = qkv[..., ATTN_WIDTH:2 * ATTN_WIDTH].reshape(B, S, N_HEADS, 2, DIFF_HEAD_DIM)
    v = qkv[..., 2 * ATTN_WIDTH:].reshape(B, S, N_HEADS, DIFF_V_DIM)
    q1, q2 = q[..., 0, :], q[..., 1, :]
    k1, k2 = k[..., 0, :], k[..., 1, :]
    lam = (jnp.exp(jnp.sum(lq1.astype(jnp.float32) * lk1.astype(jnp.float32)))
           - jnp.exp(jnp.sum(lq2.astype(jnp.float32) * lk2.astype(jnp.float32)))
           + lam_init)
    scale = DIFF_HEAD_DIM ** -0.5
    nb = S // Q_BLOCK
    kpos = jnp.arange(S)

    def to_blocks(t):
        return t.reshape(B, nb, Q_BLOCK, N_HEADS, DIFF_HEAD_DIM).transpose(1, 0, 2, 3, 4)

    def block(args):
        q1b, q2b, i = args
        qpos = i * Q_BLOCK + jnp.arange(Q_BLOCK)
        rel = qpos[:, None] - kpos[None, :]
        bias = rel_bias[t5_bucket(rel)].transpose(2, 0, 1).astype(jnp.float32)
        mask = rel >= 0

        def probs(qb, kk):
            s = jnp.einsum('bqhd,bkhd->bhqk', qb, kk).astype(jnp.float32) * scale + bias
            s = jnp.where(mask, s, -jnp.inf)
            return jax.nn.softmax(s, axis=-1)

        a = probs(q1b, k1) - lam * probs(q2b, k2)
        return jnp.einsum('bhqk,bkhe->bqhe', a.astype(v.dtype), v)

    o = lax.map(block, (to_blocks(q1), to_blocks(q2), jnp.arange(nb)))
    o = o.transpose(1, 0, 2, 3, 4).reshape(B, S, N_HEADS, DIFF_V_DIM)
    o = rmsnorm(o, gain) * (1.0 - lam_init)
    return o.reshape(B, S, ATTN_WIDTH)


def _scan_combine(e1, e2):
    a1r, a1i, b1r, b1i = e1
    a2r, a2i, b2r, b2i = e2
    ar = a2r * a1r - a2i * a1i
    ai = a2r * a1i + a2i * a1r
    br = a2r * b1r - a2i * b1i + b2r
    bi = a2r * b1i + a2i * b1r + b2i
    return (ar, ai, br, bi)


def s5_layer(u, lam_re, lam_im, log_dt, b_re, b_im, c_re, c_im, d, w_glu, b_glu):
    B, S, _ = u.shape
    f32 = jnp.float32
    uf = u.astype(f32).reshape(B, S, N_SSM_GROUPS, SSM_GROUP)
    dt = jnp.exp(log_dt.astype(f32))[:, None]
    lr, li = lam_re.astype(f32), lam_im.astype(f32)
    mag = jnp.exp(lr * dt)
    ar, ai = mag * jnp.cos(li * dt), mag * jnp.sin(li * dt)
    den = lr * lr + li * li
    nr, ni = ar - 1.0, ai
    gr = (nr * lr + ni * li) / den
    gi = (ni * lr - nr * li) / den
    br_, bi_ = b_re.astype(f32), b_im.astype(f32)
    bbr = gr[..., None] * br_ - gi[..., None] * bi_
    bbi = gr[..., None] * bi_ + gi[..., None] * br_
    bu_r = jnp.einsum('bsgh,gph->bsgp', uf, bbr)
    bu_i = jnp.einsum('bsgh,gph->bsgp', uf, bbi)
    a_r = jnp.broadcast_to(ar, bu_r.shape)
    a_i = jnp.broadcast_to(ai, bu_i.shape)
    _, _, s_r, s_i = lax.associative_scan(_scan_combine, (a_r, a_i, bu_r, bu_i), axis=1)
    y = (jnp.einsum('ghp,bsgp->bsgh', c_re.astype(f32), s_r)
         - jnp.einsum('ghp,bsgp->bsgh', c_im.astype(f32), s_i)
         + d.astype(f32) * uf)
    y = jax.nn.gelu(y.reshape(B, S, SSM_WIDTH))
    y = y * jax.nn.sigmoid(y @ w_glu.astype(f32) + b_glu.astype(f32))
    return y.astype(u.dtype)


def setup_inputs(seed: int = 0) -> dict:
    key = jax.random.key(seed)
    ks = jax.random.split(key, 24)
    f32 = jnp.float32
    n = jnp.arange(SSM_STATE, dtype=f32)
    lam_re = -0.5 + 0.01 * jax.random.normal(ks[13], (DEPTH, N_SSM_GROUPS, SSM_STATE), f32)
    lam_im = math.pi * n + 0.01 * jax.random.normal(ks[14], (DEPTH, N_SSM_GROUPS, SSM_STATE), f32)
    log_dt = jax.random.uniform(ks[15], (DEPTH, N_SSM_GROUPS), f32,
                                math.log(DT_MIN), math.log(DT_MAX))
    b_scale = (2.0 * SSM_GROUP) ** -0.5
    c_scale = (2.0 * SSM_STATE) ** -0.5
    return {
        "x": jax.random.normal(ks[0], (BATCH, SEQ, D_MODEL), f32),
        "w_in": jax.random.normal(ks[1], (DEPTH, D_MODEL, IN_WIDTH), f32) * D_MODEL ** -0.5,
        "w_out": jax.random.normal(ks[2], (DEPTH, MIX_WIDTH, D_MODEL), f32) * MIX_WIDTH ** -0.5,
        "norm_mix": 1.0 + 0.02 * jax.random.normal(ks[3], (DEPTH, D_MODEL), f32),
        "norm_mlp": 1.0 + 0.02 * jax.random.normal(ks[4], (DEPTH, D_MODEL), f32),
        "w_up": jax.random.normal(ks[5], (DEPTH, D_MODEL, D_FF), f32) * D_MODEL ** -0.5,
        "w_down": jax.random.normal(ks[6], (DEPTH, D_FF, D_MODEL), f32) * D_FF ** -0.5,
        "norm_final": 1.0 + 0.02 * jax.random.normal(ks[7], (D_MODEL,), f32),
        "rel_bias": 0.5 * jax.random.normal(ks[8], (NUM_BUCKETS, N_HEADS), f32),
        "lambda_q1": 0.1 * jax.random.normal(ks[9], (DEPTH, DIFF_HEAD_DIM), f32),
        "lambda_k1": 0.1 * jax.random.normal(ks[10], (DEPTH, DIFF_HEAD_DIM), f32),
        "lambda_q2": 0.1 * jax.random.normal(ks[11], (DEPTH, DIFF_HEAD_DIM), f32),
        "lambda_k2": 0.1 * jax.random.normal(ks[12], (DEPTH, DIFF_HEAD_DIM), f32),
        "subln_gain": 1.0 + 0.02 * jax.random.normal(ks[16], (DEPTH, DIFF_V_DIM), f32),
        "ssm_lambda_re": lam_re,
        "ssm_lambda_im": lam_im,
        "ssm_log_dt": log_dt,
        "ssm_b_re": b_scale * jax.random.normal(ks[17], (DEPTH, N_SSM_GROUPS, SSM_STATE, SSM_GROUP), f32),
        "ssm_b_im": b_scale * jax.random.normal(ks[18], (DEPTH, N_SSM_GROUPS, SSM_STATE, SSM_GROUP), f32),
        "ssm_c_re": c_scale * jax.random.normal(ks[19], (DEPTH, N_SSM_GROUPS, SSM_GROUP, SSM_STATE), f32),
        "ssm_c_im": c_scale * jax.random.normal(ks[20], (DEPTH, N_SSM_GROUPS, SSM_GROUP, SSM_STATE), f32),
        "ssm_d": jax.random.normal(ks[21], (DEPTH, N_SSM_GROUPS, SSM_GROUP), f32),
        "w_glu": jax.random.normal(ks[22], (DEPTH, SSM_WIDTH, SSM_WIDTH), f32) * SSM_WIDTH ** -0.5,
        "b_glu": 0.02 * jax.random.normal(ks[23], (DEPTH, SSM_WIDTH), f32),
    }


def reference(x, w_in, w_out, norm_mix, norm_mlp, w_up, w_down, norm_final, rel_bias,
              lambda_q1, lambda_k1, lambda_q2, lambda_k2, subln_gain,
              ssm_lambda_re, ssm_lambda_im, ssm_log_dt, ssm_b_re, ssm_b_im,
              ssm_c_re, ssm_c_im, ssm_d, w_glu, b_glu):
    h = x
    for i in range(DEPTH):
        lam_init = 0.8 - 0.6 * math.exp(-0.3 * i)
        xn = rmsnorm(h, norm_mix[i])
        proj = xn @ w_in[i]
        attn = diff_attention(proj[..., :3 * ATTN_WIDTH], rel_bias,
                              lambda_q1[i], lambda_k1[i], lambda_q2[i], lambda_k2[i],
                              subln_gain[i], lam_init)
        ssm = s5_layer(proj[..., 3 * ATTN_WIDTH:], ssm_lambda_re[i], ssm_lambda_im[i],
                       ssm_log_dt[i], ssm_b_re[i], ssm_b_im[i], ssm_c_re[i], ssm_c_im[i],
                       ssm_d[i], w_glu[i], b_glu[i])
        h = h + jnp.concatenate([attn, ssm], axis=-1) @ w_out[i]
        xn = rmsnorm(h, norm_mlp[i])
        h = h + jnp.square(jax.nn.relu(xn @ w_up[i])) @ w_down[i]
    return rmsnorm(h, norm_final)
```

```python
import functools
import math

import numpy as np
import jax
import jax.numpy as jnp
from jax import lax
from jax.experimental import pallas as pl
from jax.experimental.pallas import tpu as pltpu

F32 = jnp.float32
BF16 = jnp.bfloat16

EPS = 1e-6
MAX_EXACT = 16
MAX_DISTANCE = 128
DIFF_HEAD_DIM = 64
SSM_GROUP = 16
SSM_CHUNK = 32
NEG_BIG = -1e30

V7X_VMEM_LIMIT_BYTES = 56 * 1024 * 1024
LANES = 128


def _cparams(*sem):
    return pltpu.CompilerParams(dimension_semantics=sem, vmem_limit_bytes=V7X_VMEM_LIMIT_BYTES)


def _tile(dim, pref):
    t = min(dim, pref)
    while dim % t:
        t //= 2
    return t


def _rmsnorm_kernel(x_ref, g_ref, o_ref):
    x = x_ref[...].astype(F32)
    ms = jnp.mean(x * x, axis=-1, keepdims=True)
    o_ref[...] = ((x * lax.rsqrt(ms + EPS)) * g_ref[...]).astype(o_ref.dtype)


def rmsnorm(x, g, out_dtype):
    m, d = x.shape
    tm = _tile(m, 256)
    return pl.pallas_call(
        _rmsnorm_kernel,
        grid=(m // tm,),
        in_specs=[pl.BlockSpec((tm, d), lambda i: (i, 0)),
                  pl.BlockSpec((1, d), lambda i: (0, 0))],
        out_specs=pl.BlockSpec((tm, d), lambda i: (i, 0)),
        out_shape=jax.ShapeDtypeStruct((m, d), out_dtype),
        compiler_params=_cparams("parallel"),
        name="rmsnorm",
    )(x, g.reshape(1, d).astype(F32))


def _mm_kernel(a_ref, b_ref, *rest, nk, epilogue):
    if epilogue == "residual":
        r_ref, o_ref = rest[0], rest[1]
        rest = rest[2:]
    else:
        r_ref, o_ref = None, rest[0]
        rest = rest[1:]

    def finish(acc):
        if epilogue == "relu2":
            acc = jnp.square(jnp.maximum(acc, 0.0))
        elif epilogue == "residual":
            acc = r_ref[...] + acc
        o_ref[...] = acc.astype(o_ref.dtype)

    part = jnp.dot(a_ref[...], b_ref[...], preferred_element_type=F32)
    if nk == 1:
        finish(part)
        return
    acc_ref = rest[0]
    k = pl.program_id(2)

    @pl.when(k == 0)
    def _():
        acc_ref[...] = part

    @pl.when(jnp.logical_and(k > 0, k < nk - 1))
    def _():
        acc_ref[...] += part

    @pl.when(k == nk - 1)
    def _():
        finish(acc_ref[...] + part)


def matmul(a, b, *, out_dtype, epilogue="none", res=None, tm=1024, tn=1024, tk=4096, name="matmul"):
    m, kdim = a.shape
    _, n = b.shape
    tm, tn, tk = _tile(m, tm), _tile(n, tn), _tile(kdim, tk)
    nk = kdim // tk
    in_specs = [pl.BlockSpec((tm, tk), lambda i, j, k: (i, k)),
                pl.BlockSpec((tk, tn), lambda i, j, k: (k, j))]
    args = [a, b]
    if epilogue == "residual":
        in_specs.append(pl.BlockSpec((tm, tn), lambda i, j, k: (i, j)))
        args.append(res)
    scratch = [pltpu.VMEM((tm, tn), F32)] if nk > 1 else []
    return pl.pallas_call(
        functools.partial(_mm_kernel, nk=nk, epilogue=epilogue),
        grid=(m // tm, n // tn, nk),
        in_specs=in_specs,
        out_specs=pl.BlockSpec((tm, tn), lambda i, j, k: (i, j)),
        out_shape=jax.ShapeDtypeStruct((m, n), out_dtype),
        scratch_shapes=scratch,
        compiler_params=_cparams("parallel", "parallel", "arbitrary"),
        name=name,
    )(*args)


def _outproj_kernel(a1_ref, a2_ref, b1_ref, b2_ref, r_ref, o_ref):
    acc = jnp.dot(a1_ref[...], b1_ref[...], preferred_element_type=F32)
    acc = acc + jnp.dot(a2_ref[...], b2_ref[...], preferred_element_type=F32)
    o_ref[...] = r_ref[...] + acc


def outproj(a1, a2, b1, b2, res, *, tm=1024, tn=1024):
    m, k1 = a1.shape
    _, k2 = a2.shape
    _, n = b1.shape
    tm, tn = _tile(m, tm), _tile(n, tn)
    return pl.pallas_call(
        _outproj_kernel,
        grid=(m // tm, n // tn),
        in_specs=[pl.BlockSpec((tm, k1), lambda i, j: (i, 0)),
                  pl.BlockSpec((tm, k2), lambda i, j: (i, 0)),
                  pl.BlockSpec((k1, tn), lambda i, j: (0, j)),
                  pl.BlockSpec((k2, tn), lambda i, j: (0, j)),
                  pl.BlockSpec((tm, tn), lambda i, j: (i, j))],
        out_specs=pl.BlockSpec((tm, tn), lambda i, j: (i, j)),
        out_shape=jax.ShapeDtypeStruct((m, n), F32),
        compiler_params=_cparams("parallel", "parallel"),
        name="outproj",
    )(a1, a2, b1, b2, res)


def _glu_kernel(y_ref, w_ref, b_ref, o_ref):
    y = y_ref[...]
    z = jnp.dot(y.astype(BF16), w_ref[...], preferred_element_type=F32) + b_ref[...]
    o_ref[...] = (y * jax.nn.sigmoid(z)).astype(o_ref.dtype)


def glu(y, w, b, *, tm=512):
    m, n = y.shape
    tm = _tile(m, tm)
    return pl.pallas_call(
        _glu_kernel,
        grid=(m // tm,),
        in_specs=[pl.BlockSpec((tm, n), lambda i: (i, 0)),
                  pl.BlockSpec((n, n), lambda i: (0, 0)),
                  pl.BlockSpec((1, n), lambda i: (0, 0))],
        out_specs=pl.BlockSpec((tm, n), lambda i: (i, 0)),
        out_shape=jax.ShapeDtypeStruct((m, n), BF16),
        compiler_params=_cparams("parallel"),
        name="glu",
    )(y, w, b.reshape(1, n).astype(F32))


def _t5_bucket_np(n, num_buckets):
    n = np.maximum(n, 0)
    nf = np.maximum(n, 1).astype(np.float32)
    large = MAX_EXACT + (np.log(nf / np.float32(MAX_EXACT)) / np.float32(math.log(MAX_DISTANCE / MAX_EXACT))
                         * np.float32(num_buckets - MAX_EXACT)).astype(np.int32)
    large = np.minimum(large, num_buckets - 1)
    return np.where(n < MAX_EXACT, n, large).astype(np.int32)


def _bias_tiles_kernel(rb_ref, idx_ref, o_ref, *, num_buckets):
    h = pl.program_id(0)
    for d in range(2):
        idx = idx_ref[d]
        acc = jnp.full(idx.shape, NEG_BIG, F32)
        for k in range(num_buckets):
            acc = jnp.where(idx == k, rb_ref[k, h], acc)
        o_ref[0, d] = acc


def bias_tiles(rel_bias, t):
    num_buckets, n_heads = rel_bias.shape
    r = np.arange(t)[:, None]
    c = np.arange(t)[None, :]
    rel0 = r - c
    idx0 = np.where(rel0 >= 0, _t5_bucket_np(rel0, num_buckets), -1)
    idx1 = _t5_bucket_np(t + r - c, num_buckets)
    idx = jnp.asarray(np.stack([idx0, idx1]).astype(np.int32))
    return pl.pallas_call(
        functools.partial(_bias_tiles_kernel, num_buckets=num_buckets),
        grid=(n_heads,),
        in_specs=[pl.BlockSpec(memory_space=pltpu.SMEM),
                  pl.BlockSpec((2, t, t), lambda h: (0, 0, 0))],
        out_specs=pl.BlockSpec((1, 2, t, t), lambda h: (h, 0, 0, 0)),
        out_shape=jax.ShapeDtypeStruct((n_heads, 2, t, t), F32),
        compiler_params=_cparams("arbitrary"),
        name="bias_tiles",
    )(rel_bias.astype(F32), idx)


def _attn_kernel(rb_ref, lamv_ref, gain_ref, bias_ref, q_ref, k_ref, v_ref, o_ref,
                 m_ref, l_ref, acc_ref, *, t, lam_init, far_bucket):
    h = pl.program_id(1)
    qi = pl.program_id(2)
    d = DIFF_HEAD_DIM
    scale = d ** -0.5

    q = q_ref[...] * jnp.asarray(scale, q_ref.dtype)
    qs = (q[:, :d], q[:, d:])

    m_ref[...] = jnp.full(m_ref.shape, NEG_BIG, F32)
    l_ref[...] = jnp.zeros(l_ref.shape, F32)
    acc_ref[...] = jnp.zeros(acc_ref.shape, F32)

    def update(j, bias):
        start = pl.multiple_of(j * t, t)
        kblk = k_ref[pl.ds(start, t), :]
        vblk = v_ref[pl.ds(start, t), :]
        for i in range(2):
            s = lax.dot_general(qs[i], kblk[:, i * d:(i + 1) * d], (((1,), (1,)), ((), ())),
                                preferred_element_type=F32) + bias
            m_prev = m_ref[i]
            m_new = jnp.maximum(m_prev, jnp.max(s, axis=-1, keepdims=True))
            alpha = jnp.exp(m_prev - m_new)
            p = jnp.exp(s - m_new)
            l_ref[i] = alpha * l_ref[i] + jnp.sum(p, axis=-1, keepdims=True)
            acc_ref[i] = alpha * acc_ref[i] + jnp.dot(p.astype(v_ref.dtype), vblk,
                                                      preferred_element_type=F32)
            m_ref[i] = m_new

    far = rb_ref[far_bucket, h]

    def far_body(j, carry):
        update(j, far)
        return carry

    lax.fori_loop(0, jnp.maximum(qi - 1, 0), far_body, 0)

    @pl.when(qi >= 1)
    def _():
        update(qi - 1, bias_ref[0, 1])

    update(qi, bias_ref[0, 0])

    lv = lamv_ref[...]
    lam = (jnp.exp(jnp.sum(lv[0:1] * lv[1:2], axis=-1, keepdims=True))
           - jnp.exp(jnp.sum(lv[2:3] * lv[3:4], axis=-1, keepdims=True)) + lam_init)
    o = acc_ref[0] / l_ref[0] - lam * (acc_ref[1] / l_ref[1])
    o = o * lax.rsqrt(jnp.mean(o * o, axis=-1, keepdims=True) + EPS)
    o_ref[...] = ((o * gain_ref[...]) * (1.0 - lam_init)).astype(o_ref.dtype)


def diff_attention(qkv, bias, rel_bias, lamv, gain, *, batch, seq, lam_init, t):
    n_heads = rel_bias.shape[1]
    hd = 2 * DIFF_HEAD_DIM
    nq = seq // t
    kernel = functools.partial(_attn_kernel, t=t, lam_init=lam_init, far_bucket=rel_bias.shape[0] - 1)
    return pl.pallas_call(
        kernel,
        grid=(batch, n_heads, nq),
        in_specs=[pl.BlockSpec(memory_space=pltpu.SMEM),
                  pl.BlockSpec((4, DIFF_HEAD_DIM), lambda b, h, i: (0, 0)),
                  pl.BlockSpec((1, hd), lambda b, h, i: (0, 0)),
                  pl.BlockSpec((1, 2, t, t), lambda b, h, i: (h, 0, 0, 0)),
                  pl.BlockSpec((t, hd), lambda b, h, i: (b * nq + i, h)),
                  pl.BlockSpec((seq, hd), lambda b, h, i: (b, n_heads + h)),
                  pl.BlockSpec((seq, hd), lambda b, h, i: (b, 2 * n_heads + h))],
        out_specs=pl.BlockSpec((t, hd), lambda b, h, i: (b * nq + i, h)),
        out_shape=jax.ShapeDtypeStruct((batch * seq, n_heads * hd), BF16),
        scratch_shapes=[pltpu.VMEM((2, t, 1), F32), pltpu.VMEM((2, t, 1), F32),
                        pltpu.VMEM((2, t, hd), F32)],
        compiler_params=_cparams("parallel", "parallel", "arbitrary"),
        name="diff_attention",
    )(rel_bias.astype(F32), lamv, gain.reshape(1, hd).astype(F32), bias, qkv, qkv, qkv)


def _s5_prep_kernel(lre_ref, lim_ref, ldt_ref, btr_ref, bti_ref, cr_ref, ci_ref,
                    tg_ref, wg_ref, vgt_ref, ap_ref, e_ref, q_ref, *, gb, chunk, nlev):
    L = chunk
    P = lre_ref.shape[-1]
    W = SSM_GROUP * L
    rows = e_ref.shape[1]
    lane = lax.broadcasted_iota(jnp.int32, (1, 2 * P), 1)
    sgn = jnp.where(lane < P, -1.0, 1.0).astype(F32)
    jj = lax.broadcasted_iota(jnp.int32, (rows, 2 * P), 0).astype(F32)

    def dup(x):
        return jnp.concatenate([x, x], axis=-1)

    def group(g, carry):
        dt = jnp.exp(ldt_ref[g])
        lr, li = lre_ref[g], lim_ref[g]
        lrdt, lidt = lr * dt, li * dt
        mag = jnp.exp(lrdt)
        ar, ai = mag * jnp.cos(lidt), mag * jnp.sin(lidt)
        den = lr * lr + li * li
        nr, ni = ar - 1.0, ai
        gr = (nr * lr + ni * li) / den
        gi = (ni * lr - nr * li) / den
        btr, bti = btr_ref[g], bti_ref[g]
        bbr = gr * btr - gi * bti
        bbi = gr * bti + gi * btr
        cr, ci = cr_ref[g], ci_ref[g]
        bc1 = jnp.concatenate([bbr, bbi], axis=-1)
        bc2 = jnp.concatenate([-bbi, bbr], axis=-1)
        cc1 = jnp.concatenate([cr, -ci], axis=-1)
        cc2 = jnp.concatenate([-ci, -cr], axis=-1)

        mg = jnp.exp(jj * dup(lrdt))
        ang = jj * dup(lidt)
        e_ref[0] = mg * jnp.cos(ang)
        e_ref[1] = mg * jnp.sin(ang)

        for j in range(L + 1):
            er = jnp.broadcast_to(e_ref[0, j:j + 1, :], (SSM_GROUP, 2 * P))
            ei = jnp.broadcast_to(e_ref[1, j:j + 1, :], (SSM_GROUP, 2 * P))
            q_ref[j * SSM_GROUP:(j + 1) * SSM_GROUP, :] = er * cc1 + ei * cc2
            if j < L:
                tau = L - 1 - j
                wg_ref[g, tau * SSM_GROUP:(tau + 1) * SSM_GROUP, :] = (er * bc1 + ei * bc2).astype(wg_ref.dtype)

        vgt_ref[g] = q_ref[SSM_GROUP:SSM_GROUP * (L + 1), :].astype(vgt_ref.dtype)
        strip = lax.dot_general(bc1, q_ref[0:W, :], (((1,), (1,)), ((), ())),
                                precision=lax.Precision.HIGHEST, preferred_element_type=F32)
        pad = jnp.concatenate([jnp.zeros_like(strip), strip], axis=-1)
        lanes_per_step = LANES // SSM_GROUP
        for r in range(lanes_per_step):
            rolled = pad if r == 0 else pltpu.roll(pad, SSM_GROUP * r, axis=1)
            for qd in range(L // lanes_per_step):
                tau = lanes_per_step * qd + r
                tg_ref[g, tau * SSM_GROUP:(tau + 1) * SSM_GROUP, :] = (
                    rolled[:, W - LANES * qd:2 * W - LANES * qd].astype(tg_ref.dtype))

        pr, pi = e_ref[0, L:L + 1, :], e_ref[1, L:L + 1, :]
        for k in range(nlev):
            ap_ref[g, k, 0:1, :] = pr
            ap_ref[g, k, 1:2, :] = pi * sgn
            pr, pi = pr * pr - pi * pi, 2.0 * pr * pi
        return carry

    lax.fori_loop(0, gb, group, 0)


def s5_prep(lam_re, lam_im, log_dt, bt_re, bt_im, c_re, c_im, *, chunk, nlev, gb=8):
    G, P = lam_re.shape
    H = SSM_GROUP
    W = H * chunk
    assert 2 * P == LANES and chunk % (LANES // H) == 0
    gb = _tile(G, gb)
    rows = -(-(chunk + 1) // 8) * 8
    vec = lambda a: a.reshape(G, 1, -1).astype(F32)
    g3 = lambda s1, s2: pl.BlockSpec((gb, s1, s2), lambda i: (i, 0, 0))
    return pl.pallas_call(
        functools.partial(_s5_prep_kernel, gb=gb, chunk=chunk, nlev=nlev),
        grid=(G // gb,),
        in_specs=[g3(1, P), g3(1, P), g3(1, 1), g3(H, P), g3(H, P), g3(H, P), g3(H, P)],
        out_specs=[g3(W, W), g3(W, 2 * P), g3(W, 2 * P),
                   pl.BlockSpec((gb, nlev, 2, 2 * P), lambda i: (i, 0, 0, 0))],
        out_shape=[jax.ShapeDtypeStruct((G, W, W), BF16),
                   jax.ShapeDtypeStruct((G, W, 2 * P), BF16),
                   jax.ShapeDtypeStruct((G, W, 2 * P), BF16),
                   jax.ShapeDtypeStruct((G, nlev, 2, 2 * P), F32)],
        scratch_shapes=[pltpu.VMEM((2, rows, 2 * P), F32),
                        pltpu.VMEM((H * (chunk + 1), 2 * P), F32)],
        compiler_params=_cparams("parallel"),
        name="s5_prep",
    )(vec(lam_re), vec(lam_im), vec(log_dt), bt_re.astype(F32), bt_im.astype(F32),
      c_re.astype(F32), c_im.astype(F32))


def _s5_kernel(x_ref, tg_ref, wg_ref, vgt_ref, ap_ref, d_ref, o_ref, *, gb, nc, nlev):
    n = x_ref.shape[1]
    P2 = wg_ref.shape[-1]
    cidx = lax.broadcasted_iota(jnp.int32, (n, P2), 0) % nc

    def shift_rows(s, sh):
        return jnp.where(cidx >= sh, pltpu.roll(s, sh, axis=0), 0.0)

    def group(g, carry):
        x = x_ref[g]
        xb = x.astype(BF16)
        s = jnp.dot(xb, wg_ref[g], preferred_element_type=F32)
        for k in range(nlev):
            sh = shift_rows(s, 1 << k)
            s = s + sh * ap_ref[g, k, 0:1, :] + pltpu.roll(sh, P2 // 2, axis=1) * ap_ref[g, k, 1:2, :]
        s_in = shift_rows(s, 1)
        y = jnp.dot(xb, tg_ref[g], preferred_element_type=F32)
        y = y + lax.dot_general(s_in.astype(BF16), vgt_ref[g], (((1,), (1,)), ((), ())),
                                preferred_element_type=F32)
        y = y + x * d_ref[g]
        o_ref[g] = jax.nn.gelu(y)
        return carry

    lax.fori_loop(0, gb, group, 0)


def s5_apply(xg, tg, wg, vgt, ap, dt, *, nc, nlev, gb=8):
    G, n, W = xg.shape
    gb = _tile(G, gb)
    g3 = lambda s1, s2: pl.BlockSpec((gb, s1, s2), lambda i: (i, 0, 0))
    return pl.pallas_call(
        functools.partial(_s5_kernel, gb=gb, nc=nc, nlev=nlev),
        grid=(G // gb,),
        in_specs=[g3(n, W), g3(W, W), g3(W, LANES), g3(W, LANES),
                  pl.BlockSpec((gb, nlev, 2, LANES), lambda i: (i, 0, 0, 0)), g3(1, W)],
        out_specs=g3(n, W),
        out_shape=jax.ShapeDtypeStruct((G, n, W), F32),
        compiler_params=_cparams("parallel"),
        name="s5_apply",
    )(xg, tg, wg, vgt, ap, dt)


def s5_layer(u, lam_re, lam_im, log_dt, b_re, b_im, c_re, c_im, d, *, batch, seq):
    T, width = u.shape
    G = width // SSM_GROUP
    L = _tile(seq, SSM_CHUNK)
    nc = seq // L
    nlev = max(nc - 1, 0).bit_length()
    n = batch * nc
    W = SSM_GROUP * L
    bt_re = jnp.swapaxes(b_re, -1, -2)
    bt_im = jnp.swapaxes(b_im, -1, -2)
    tg, wg, vgt, ap = s5_prep(lam_re, lam_im, log_dt, bt_re, bt_im, c_re, c_im, chunk=L, nlev=max(nlev, 1))
    xg = u.reshape(n, L, G, SSM_GROUP).transpose(2, 0, 1, 3).reshape(G, n, W)
    dtile = jnp.tile(d.astype(F32), (1, L)).reshape(G, 1, W)
    yg = s5_apply(xg, tg, wg, vgt, ap, dtile, nc=nc, nlev=nlev)
    return yg.reshape(G, n, L, SSM_GROUP).transpose(1, 2, 0, 3).reshape(T, width)


@jax.jit
def _forward(x, w_in, w_out, norm_mix, norm_mlp, w_up, w_down, norm_final, rel_bias,
             lambda_q1, lambda_k1, lambda_q2, lambda_k2, subln_gain,
             ssm_lambda_re, ssm_lambda_im, ssm_log_dt, ssm_b_re, ssm_b_im,
             ssm_c_re, ssm_c_im, ssm_d, w_glu, b_glu):
    batch, seq, d_model = x.shape
    depth = w_in.shape[0]
    ssm_width = w_glu.shape[-1]
    attn_width = (w_in.shape[-1] - ssm_width) // 3
    T = batch * seq

    t_attn = _tile(seq, 256)
    num_buckets = rel_bias.shape[0]
    far = _t5_bucket_np(np.arange(t_attn + 1, max(seq, t_attn + 2)), num_buckets)
    assert np.all(far == num_buckets - 1), "kv tiles two or more left of the diagonal must share one bucket"
    bias = bias_tiles(rel_bias, t_attn)

    h = x.reshape(T, d_model).astype(F32)
    for i in range(depth):
        lam_init = 0.8 - 0.6 * math.exp(-0.3 * i)
        w_qkv = w_in[i, :, :3 * attn_width].astype(BF16)
        w_u = w_in[i, :, 3 * attn_width:].astype(BF16)
        xn = rmsnorm(h, norm_mix[i], BF16)
        qkv = matmul(xn, w_qkv, out_dtype=BF16, name="proj_qkv")
        u = matmul(xn, w_u, out_dtype=F32, name="proj_u")
        lamv = jnp.stack([lambda_q1[i], lambda_k1[i], lambda_q2[i], lambda_k2[i]]).astype(F32)
        attn = diff_attention(qkv, bias, rel_bias, lamv, subln_gain[i],
                              batch=batch, seq=seq, lam_init=lam_init, t=t_attn)
        y = s5_layer(u, ssm_lambda_re[i], ssm_lambda_im[i], ssm_log_dt[i], ssm_b_re[i], ssm_b_im[i],
                     ssm_c_re[i], ssm_c_im[i], ssm_d[i], batch=batch, seq=seq)
        ssm = glu(y, w_glu[i].astype(BF16), b_glu[i])
        h = outproj(attn, ssm, w_out[i, :attn_width].astype(BF16), w_out[i, attn_width:].astype(BF16), h)
        xn = rmsnorm(h, norm_mlp[i], BF16)
        ff = matmul(xn, w_up[i].astype(BF16), out_dtype=BF16, epilogue="relu2", name="mlp_up")
        h = matmul(ff, w_down[i].astype(BF16), out_dtype=F32, epilogue="residual", res=h,
                   tk=2048, name="mlp_down")
    out = rmsnorm(h, norm_final, x.dtype)
    return out.reshape(batch, seq, d_model)


def kernel(x, w_in, w_out, norm_mix, norm_mlp, w_up, w_down, norm_final, rel_bias, lambda_q1, lambda_k1, lambda_q2, lambda_k2, subln_gain, ssm_lambda_re, ssm_lambda_im, ssm_log_dt, ssm_b_re, ssm_b_im, ssm_c_re, ssm_c_im, ssm_d, w_glu, b_glu):
    return _forward(x, w_in, w_out, norm_mix, norm_mlp, w_up, w_down, norm_final, rel_bias,
                    lambda_q1, lambda_k1, lambda_q2, lambda_k2, subln_gain,
                    ssm_lambda_re, ssm_lambda_im, ssm_log_dt, ssm_b_re, ssm_b_im,
                    ssm_c_re, ssm_c_im, ssm_d, w_glu, b_glu)
```

```python
import functools
import math

import numpy as np
import jax
import jax.numpy as jnp
from jax import lax
from jax.experimental import pallas as pl
from jax.experimental.pallas import tpu as pltpu

F32 = jnp.float32
BF16 = jnp.bfloat16

EPS = 1e-6
MAX_EXACT = 16
MAX_DISTANCE = 128
DIFF_HEAD_DIM = 64
SSM_GROUP = 16
SSM_CHUNK = 32
NEG_BIG = -1e30

V7X_VMEM_LIMIT_BYTES = 56 * 1024 * 1024
LANES = 128


def _cparams(*sem):
    return pltpu.CompilerParams(dimension_semantics=sem, vmem_limit_bytes=V7X_VMEM_LIMIT_BYTES)


def _tile(dim, pref):
    t = min(dim, pref)
    while dim % t:
        t //= 2
    return t


def _rmsnorm_kernel(x_ref, g_ref, o_ref):
    x = x_ref[...].astype(F32)
    ms = jnp.mean(x * x, axis=-1, keepdims=True)
    o_ref[...] = ((x * lax.rsqrt(ms + EPS)) * g_ref[...]).astype(o_ref.dtype)


def rmsnorm(x, g, out_dtype):
    m, d = x.shape
    tm = _tile(m, 256)
    return pl.pallas_call(
        _rmsnorm_kernel,
        grid=(m // tm,),
        in_specs=[pl.BlockSpec((tm, d), lambda i: (i, 0)),
                  pl.BlockSpec((1, d), lambda i: (0, 0))],
        out_specs=pl.BlockSpec((tm, d), lambda i: (i, 0)),
        out_shape=jax.ShapeDtypeStruct((m, d), out_dtype),
        compiler_params=_cparams("parallel"),
        name="rmsnorm",
    )(x, g.reshape(1, d).astype(F32))


def _mm_kernel(a_ref, b_ref, *rest, nk, epilogue):
    if epilogue == "residual":
        r_ref, o_ref = rest[0], rest[1]
        rest = rest[2:]
    else:
        r_ref, o_ref = None, rest[0]
        rest = rest[1:]

    def finish(acc):
        if epilogue == "relu2":
            acc = jnp.square(jnp.maximum(acc, 0.0))
        elif epilogue == "residual":
            acc = r_ref[...] + acc
        o_ref[...] = acc.astype(o_ref.dtype)

    part = jnp.dot(a_ref[...], b_ref[...], preferred_element_type=F32)
    if nk == 1:
        finish(part)
        return
    acc_ref = rest[0]
    k = pl.program_id(2)

    @pl.when(k == 0)
    def _():
        acc_ref[...] = part

    @pl.when(jnp.logical_and(k > 0, k < nk - 1))
    def _():
        acc_ref[...] += part

    @pl.when(k == nk - 1)
    def _():
        finish(acc_ref[...] + part)


def matmul(a, b, *, out_dtype, epilogue="none", res=None, tm=1024, tn=1024, tk=4096, name="matmul"):
    m, kdim = a.shape
    _, n = b.shape
    tm, tn, tk = _tile(m, tm), _tile(n, tn), _tile(kdim, tk)
    nk = kdim // tk
    in_specs = [pl.BlockSpec((tm, tk), lambda i, j, k: (i, k)),
                pl.BlockSpec((tk, tn), lambda i, j, k: (k, j))]
    args = [a, b]
    if epilogue == "residual":
        in_specs.append(pl.BlockSpec((tm, tn), lambda i, j, k: (i, j)))
        args.append(res)
    scratch = [pltpu.VMEM((tm, tn), F32)] if nk > 1 else []
    return pl.pallas_call(
        functools.partial(_mm_kernel, nk=nk, epilogue=epilogue),
        grid=(m // tm, n // tn, nk),
        in_specs=in_specs,
        out_specs=pl.BlockSpec((tm, tn), lambda i, j, k: (i, j)),
        out_shape=jax.ShapeDtypeStruct((m, n), out_dtype),
        scratch_shapes=scratch,
        compiler_params=_cparams("parallel", "parallel", "arbitrary"),
        name=name,
    )(*args)


def _outproj_kernel(a1_ref, a2_ref, b1_ref, b2_ref, r_ref, o_ref):
    acc = jnp.dot(a1_ref[...], b1_ref[...], preferred_element_type=F32)
    acc = acc + jnp.dot(a2_ref[...], b2_ref[...], preferred_element_type=F32)
    o_ref[...] = r_ref[...] + acc


def outproj(a1, a2, b1, b2, res, *, tm=1024, tn=1024):
    m, k1 = a1.shape
    _, k2 = a2.shape
    _, n = b1.shape
    tm, tn = _tile(m, tm), _tile(n, tn)
    return pl.pallas_call(
        _outproj_kernel,
        grid=(m // tm, n // tn),
        in_specs=[pl.BlockSpec((tm, k1), lambda i, j: (i, 0)),
                  pl.BlockSpec((tm, k2), lambda i, j: (i, 0)),
                  pl.BlockSpec((k1, tn), lambda i, j: (0, j)),
                  pl.BlockSpec((k2, tn), lambda i, j: (0, j)),
                  pl.BlockSpec((tm, tn), lambda i, j: (i, j))],
        out_specs=pl.BlockSpec((tm, tn), lambda i, j: (i, j)),
        out_shape=jax.ShapeDtypeStruct((m, n), F32),
        compiler_params=_cparams("parallel", "parallel"),
        name="outproj",
    )(a1, a2, b1, b2, res)


def _glu_kernel(y_ref, w_ref, b_ref, o_ref):
    y = y_ref[...]
    z = jnp.dot(y.astype(BF16), w_ref[...], preferred_element_type=F32) + b_ref[...]
    o_ref[...] = (y * jax.nn.sigmoid(z)).astype(o_ref.dtype)


def glu(y, w, b, *, tm=512):
    m, n = y.shape
    tm = _tile(m, tm)
    return pl.pallas_call(
        _glu_kernel,
        grid=(m // tm,),
        in_specs=[pl.BlockSpec((tm, n), lambda i: (i, 0)),
                  pl.BlockSpec((n, n), lambda i: (0, 0)),
                  pl.BlockSpec((1, n), lambda i: (0, 0))],
        out_specs=pl.BlockSpec((tm, n), lambda i: (i, 0)),
        out_shape=jax.ShapeDtypeStruct((m, n), BF16),
        compiler_params=_cparams("parallel"),
        name="glu",
    )(y, w, b.reshape(1, n).astype(F32))


def _t5_bucket_np(n, num_buckets):
    n = np.maximum(n, 0)
    nf = np.maximum(n, 1).astype(np.float32)
    large = MAX_EXACT + (np.log(nf / np.float32(MAX_EXACT)) / np.float32(math.log(MAX_DISTANCE / MAX_EXACT))
                         * np.float32(num_buckets - MAX_EXACT)).astype(np.int32)
    large = np.minimum(large, num_buckets - 1)
    return np.where(n < MAX_EXACT, n, large).astype(np.int32)


def _bias_tiles_kernel(rb_ref, idx_ref, o_ref, *, num_buckets):
    h = pl.program_id(0)
    for d in range(idx_ref.shape[0]):
        idx = idx_ref[d]
        acc = jnp.full(idx.shape, NEG_BIG, F32)
        for k in range(num_buckets):
            acc = jnp.where(idx == k, rb_ref[k, h], acc)
        o_ref[0, d] = acc


def bias_tiles(rel_bias, t, seq):
    num_buckets, n_heads = rel_bias.shape
    c = np.arange(t)[:, None]
    r = np.arange(t)[None, :]
    far = _t5_bucket_np(np.arange(t + 1, max(seq, t + 2)), num_buckets)
    assert np.all(far == far[0]), "kv tiles two or more steps left of the diagonal must share one bucket"
    idx_far = np.full((t, t), far[0])
    idx_left = _t5_bucket_np(t + r - c, num_buckets)
    idx_diag = np.where(r - c >= 0, _t5_bucket_np(r - c, num_buckets), -1)
    idx = jnp.asarray(np.stack([idx_far, idx_left, idx_diag]).astype(np.int32))
    return pl.pallas_call(
        functools.partial(_bias_tiles_kernel, num_buckets=num_buckets),
        grid=(n_heads,),
        in_specs=[pl.BlockSpec(memory_space=pltpu.SMEM),
                  pl.BlockSpec((3, t, t), lambda h: (0, 0, 0))],
        out_specs=pl.BlockSpec((1, 3, t, t), lambda h: (h, 0, 0, 0)),
        out_shape=jax.ShapeDtypeStruct((n_heads, 3, t, t), F32),
        compiler_params=_cparams("arbitrary"),
        name="bias_tiles",
    )(rel_bias.astype(F32), idx)


def _attn_kernel(lamv_ref, gain_ref, bias_ref, q_ref, k_ref, v_ref, o_ref,
                 vt_ref, qt_ref, s_ref, m_ref, l_ref, acc_ref, *, t, lam_init):
    qi = pl.program_id(2)
    d = DIFF_HEAD_DIM
    hd = 2 * d
    nkv = vt_ref.shape[0]

    @pl.when(qi == 0)
    def _():
        for c in range(nkv):
            vt_ref[c] = v_ref[c * t:(c + 1) * t, :].astype(F32).T.astype(vt_ref.dtype)

    qT = (q_ref[...].astype(F32) * (d ** -0.5)).T
    row = lax.broadcasted_iota(jnp.int32, (hd, t), 0)
    qt_ref[:, 0:t] = jnp.where(row < d, qT, 0.0).astype(qt_ref.dtype)
    qt_ref[:, t:2 * t] = jnp.where(row >= d, qT, 0.0).astype(qt_ref.dtype)

    m_ref[...] = jnp.full(m_ref.shape, NEG_BIG, F32)
    l_ref[...] = jnp.zeros(l_ref.shape, F32)
    acc_ref[...] = jnp.zeros(acc_ref.shape, F32)

    def scores(j, slot):
        kblk = k_ref[pl.ds(pl.multiple_of(j * t, t), t), :]
        s = jnp.dot(kblk, qt_ref[...], preferred_element_type=F32)
        b = bias_ref[0, jnp.clip(j - qi + 2, 0, 2)]
        s_ref[slot, :, 0:t] = s[:, 0:t] + b
        s_ref[slot, :, t:2 * t] = s[:, t:2 * t] + b

    def reduce(j, slot):
        m_prev = m_ref[...]
        m_new = jnp.maximum(m_prev, jnp.max(s_ref[slot], axis=0, keepdims=True))
        alpha = jnp.exp(m_prev - m_new)
        p = jnp.exp(s_ref[slot] - m_new)
        l_ref[...] = alpha * l_ref[...] + jnp.sum(p, axis=0, keepdims=True)
        m_ref[...] = m_new
        return alpha, p.astype(vt_ref.dtype)

    def accumulate(j, alpha, p):
        acc_ref[...] = alpha * acc_ref[...] + jnp.dot(vt_ref[j], p, preferred_element_type=F32)

    scores(0, 0)

    def body(j, carry):
        slot = j & 1
        alpha, p = reduce(j, slot)
        scores(j + 1, 1 - slot)
        accumulate(j, alpha, p)
        return carry

    lax.fori_loop(0, qi, body, 0)
    alpha, p = reduce(qi, qi & 1)
    accumulate(qi, alpha, p)

    lv = lamv_ref[...]
    lam = (jnp.exp(jnp.sum(lv[0:1] * lv[1:2], axis=-1, keepdims=True))
           - jnp.exp(jnp.sum(lv[2:3] * lv[3:4], axis=-1, keepdims=True)) + lam_init)
    on = acc_ref[...] * (1.0 / l_ref[...])
    o = on[:, 0:t] - lam * on[:, t:2 * t]
    o = o * lax.rsqrt(jnp.mean(o * o, axis=0, keepdims=True) + EPS)
    o = (o * gain_ref[...]) * (1.0 - lam_init)
    o_ref[...] = o.T.astype(o_ref.dtype)


def diff_attention(qkv, bias, rel_bias, lamv, gain, *, batch, seq, lam_init, t):
    n_heads = rel_bias.shape[1]
    hd = 2 * DIFF_HEAD_DIM
    nq = seq // t
    kernel = functools.partial(_attn_kernel, t=t, lam_init=lam_init)
    return pl.pallas_call(
        kernel,
        grid=(batch, n_heads, nq),
        in_specs=[pl.BlockSpec((4, DIFF_HEAD_DIM), lambda b, h, i: (0, 0)),
                  pl.BlockSpec((hd, 1), lambda b, h, i: (0, 0)),
                  pl.BlockSpec((1, 3, t, t), lambda b, h, i: (h, 0, 0, 0)),
                  pl.BlockSpec((t, hd), lambda b, h, i: (b * nq + i, h)),
                  pl.BlockSpec((seq, hd), lambda b, h, i: (b, n_heads + h)),
                  pl.BlockSpec((seq, hd), lambda b, h, i: (b, 2 * n_heads + h))],
        out_specs=pl.BlockSpec((t, hd), lambda b, h, i: (b * nq + i, h)),
        out_shape=jax.ShapeDtypeStruct((batch * seq, n_heads * hd), BF16),
        scratch_shapes=[pltpu.VMEM((nq, hd, t), BF16), pltpu.VMEM((hd, 2 * t), BF16),
                        pltpu.VMEM((2, t, 2 * t), F32),
                        pltpu.VMEM((1, 2 * t), F32), pltpu.VMEM((1, 2 * t), F32),
                        pltpu.VMEM((hd, 2 * t), F32)],
        compiler_params=_cparams("parallel", "parallel", "arbitrary"),
        name="diff_attention",
    )(lamv, gain.reshape(hd, 1).astype(F32), bias, qkv, qkv, qkv)


def _s5_prep_kernel(lre_ref, lim_ref, ldt_ref, btr_ref, bti_ref, cr_ref, ci_ref,
                    tg_ref, wg_ref, vgt_ref, ap_ref, e_ref, q_ref, *, gb, chunk, nlev):
    L = chunk
    P = lre_ref.shape[-1]
    W = SSM_GROUP * L
    rows = e_ref.shape[1]
    lane = lax.broadcasted_iota(jnp.int32, (1, 2 * P), 1)
    sgn = jnp.where(lane < P, -1.0, 1.0).astype(F32)
    jj = lax.broadcasted_iota(jnp.int32, (rows, 2 * P), 0).astype(F32)

    def dup(x):
        return jnp.concatenate([x, x], axis=-1)

    def group(g, carry):
        dt = jnp.exp(ldt_ref[g])
        lr, li = lre_ref[g], lim_ref[g]
        lrdt, lidt = lr * dt, li * dt
        mag = jnp.exp(lrdt)
        ar, ai = mag * jnp.cos(lidt), mag * jnp.sin(lidt)
        den = lr * lr + li * li
        nr, ni = ar - 1.0, ai
        gr = (nr * lr + ni * li) / den
        gi = (ni * lr - nr * li) / den
        btr, bti = btr_ref[g], bti_ref[g]
        bbr = gr * btr - gi * bti
        bbi = gr * bti + gi * btr
        cr, ci = cr_ref[g], ci_ref[g]
        bc1 = jnp.concatenate([bbr, bbi], axis=-1)
        bc2 = jnp.concatenate([-bbi, bbr], axis=-1)
        cc1 = jnp.concatenate([cr, -ci], axis=-1)
        cc2 = jnp.concatenate([-ci, -cr], axis=-1)

        mg = jnp.exp(jj * dup(lrdt))
        ang = jj * dup(lidt)
        e_ref[0] = mg * jnp.cos(ang)
        e_ref[1] = mg * jnp.sin(ang)

        for j in range(L + 1):
            er = jnp.broadcast_to(e_ref[0, j:j + 1, :], (SSM_GROUP, 2 * P))
            ei = jnp.broadcast_to(e_ref[1, j:j + 1, :], (SSM_GROUP, 2 * P))
            q_ref[j * SSM_GROUP:(j + 1) * SSM_GROUP, :] = er * cc1 + ei * cc2
            if j < L:
                tau = L - 1 - j
                wg_ref[g, tau * SSM_GROUP:(tau + 1) * SSM_GROUP, :] = (er * bc1 + ei * bc2).astype(wg_ref.dtype)

        vgt_ref[g] = q_ref[SSM_GROUP:SSM_GROUP * (L + 1), :].astype(vgt_ref.dtype)
        strip = lax.dot_general(bc1, q_ref[0:W, :], (((1,), (1,)), ((), ())),
                                precision=lax.Precision.HIGHEST, preferred_element_type=F32)
        pad = jnp.concatenate([jnp.zeros_like(strip), strip], axis=-1)
        lanes_per_step = LANES // SSM_GROUP
        for r in range(lanes_per_step):
            rolled = pad if r == 0 else pltpu.roll(pad, SSM_GROUP * r, axis=1)
            for qd in range(L // lanes_per_step):
                tau = lanes_per_step * qd + r
                tg_ref[g, tau * SSM_GROUP:(tau + 1) * SSM_GROUP, :] = (
                    rolled[:, W - LANES * qd:2 * W - LANES * qd].astype(tg_ref.dtype))

        pr, pi = e_ref[0, L:L + 1, :], e_ref[1, L:L + 1, :]
        for k in range(nlev):
            ap_ref[g, k, 0:1, :] = pr
            ap_ref[g, k, 1:2, :] = pi * sgn
            pr, pi = pr * pr - pi * pi, 2.0 * pr * pi
        return carry

    lax.fori_loop(0, gb, group, 0)


def s5_prep(lam_re, lam_im, log_dt, bt_re, bt_im, c_re, c_im, *, chunk, nlev, gb=8):
    G, P = lam_re.shape
    H = SSM_GROUP
    W = H * chunk
    assert 2 * P == LANES and chunk % (LANES // H) == 0
    gb = _tile(G, gb)
    rows = -(-(chunk + 1) // 8) * 8
    vec = lambda a: a.reshape(G, 1, -1).astype(F32)
    g3 = lambda s1, s2: pl.BlockSpec((gb, s1, s2), lambda i: (i, 0, 0))
    return pl.pallas_call(
        functools.partial(_s5_prep_kernel, gb=gb, chunk=chunk, nlev=nlev),
        grid=(G // gb,),
        in_specs=[g3(1, P), g3(1, P), g3(1, 1), g3(H, P), g3(H, P), g3(H, P), g3(H, P)],
        out_specs=[g3(W, W), g3(W, 2 * P), g3(W, 2 * P),
                   pl.BlockSpec((gb, nlev, 2, 2 * P), lambda i: (i, 0, 0, 0))],
        out_shape=[jax.ShapeDtypeStruct((G, W, W), BF16),
                   jax.ShapeDtypeStruct((G, W, 2 * P), BF16),
                   jax.ShapeDtypeStruct((G, W, 2 * P), BF16),
                   jax.ShapeDtypeStruct((G, nlev, 2, 2 * P), F32)],
        scratch_shapes=[pltpu.VMEM((2, rows, 2 * P), F32),
                        pltpu.VMEM((H * (chunk + 1), 2 * P), F32)],
        compiler_params=_cparams("parallel"),
        name="s5_prep",
    )(vec(lam_re), vec(lam_im), vec(log_dt), bt_re.astype(F32), bt_im.astype(F32),
      c_re.astype(F32), c_im.astype(F32))


def _s5_kernel(x_ref, tg_ref, wg_ref, vgt_ref, ap_ref, d_ref, o_ref, *, gb, nc, nlev):
    n = x_ref.shape[1]
    P2 = wg_ref.shape[-1]
    cidx = lax.broadcasted_iota(jnp.int32, (n, P2), 0) % nc

    def shift_rows(s, sh):
        return jnp.where(cidx >= sh, pltpu.roll(s, sh, axis=0), 0.0)

    def group(g, carry):
        x = x_ref[g]
        xb = x.astype(BF16)
        s = jnp.dot(xb, wg_ref[g], preferred_element_type=F32)
        for k in range(nlev):
            sh = shift_rows(s, 1 << k)
            s = s + sh * ap_ref[g, k, 0:1, :] + pltpu.roll(sh, P2 // 2, axis=1) * ap_ref[g, k, 1:2, :]
        s_in = shift_rows(s, 1)
        y = jnp.dot(xb, tg_ref[g], preferred_element_type=F32)
        y = y + lax.dot_general(s_in.astype(BF16), vgt_ref[g], (((1,), (1,)), ((), ())),
                                preferred_element_type=F32)
        y = y + x * d_ref[g]
        o_ref[g] = jax.nn.gelu(y)
        return carry

    lax.fori_loop(0, gb, group, 0)


def s5_apply(xg, tg, wg, vgt, ap, dt, *, nc, nlev, gb=8):
    G, n, W = xg.shape
    gb = _tile(G, gb)
    g3 = lambda s1, s2: pl.BlockSpec((gb, s1, s2), lambda i: (i, 0, 0))
    return pl.pallas_call(
        functools.partial(_s5_kernel, gb=gb, nc=nc, nlev=nlev),
        grid=(G // gb,),
        in_specs=[g3(n, W), g3(W, W), g3(W, LANES), g3(W, LANES),
                  pl.BlockSpec((gb, nlev, 2, LANES), lambda i: (i, 0, 0, 0)), g3(1, W)],
        out_specs=g3(n, W),
        out_shape=jax.ShapeDtypeStruct((G, n, W), F32),
        compiler_params=_cparams("parallel"),
        name="s5_apply",
    )(xg, tg, wg, vgt, ap, dt)


def s5_layer(u, lam_re, lam_im, log_dt, b_re, b_im, c_re, c_im, d, *, batch, seq):
    T, width = u.shape
    G = width // SSM_GROUP
    L = _tile(seq, SSM_CHUNK)
    nc = seq // L
    nlev = max(nc - 1, 0).bit_length()
    n = batch * nc
    W = SSM_GROUP * L
    bt_re = jnp.swapaxes(b_re, -1, -2)
    bt_im = jnp.swapaxes(b_im, -1, -2)
    tg, wg, vgt, ap = s5_prep(lam_re, lam_im, log_dt, bt_re, bt_im, c_re, c_im, chunk=L, nlev=max(nlev, 1))
    xg = u.reshape(n, L, G, SSM_GROUP).transpose(2, 0, 1, 3).reshape(G, n, W)
    dtile = jnp.tile(d.astype(F32), (1, L)).reshape(G, 1, W)
    yg = s5_apply(xg, tg, wg, vgt, ap, dtile, nc=nc, nlev=nlev)
    return yg.reshape(G, n, L, SSM_GROUP).transpose(1, 2, 0, 3).reshape(T, width)


@jax.jit
def _forward(x, w_in, w_out, norm_mix, norm_mlp, w_up, w_down, norm_final, rel_bias,
             lambda_q1, lambda_k1, lambda_q2, lambda_k2, subln_gain,
             ssm_lambda_re, ssm_lambda_im, ssm_log_dt, ssm_b_re, ssm_b_im,
             ssm_c_re, ssm_c_im, ssm_d, w_glu, b_glu):
    batch, seq, d_model = x.shape
    depth = w_in.shape[0]
    ssm_width = w_glu.shape[-1]
    attn_width = (w_in.shape[-1] - ssm_width) // 3
    T = batch * seq

    t_attn = _tile(seq, 256)
    bias = bias_tiles(rel_bias, t_attn, seq)

    h = x.reshape(T, d_model).astype(F32)
    for i in range(depth):
        lam_init = 0.8 - 0.6 * math.exp(-0.3 * i)
        w_qkv = w_in[i, :, :3 * attn_width].astype(BF16)
        w_u = w_in[i, :, 3 * attn_width:].astype(BF16)
        xn = rmsnorm(h, norm_mix[i], BF16)
        qkv = matmul(xn, w_qkv, out_dtype=BF16, name="proj_qkv")
        u = matmul(xn, w_u, out_dtype=F32, name="proj_u")
        lamv = jnp.stack([lambda_q1[i], lambda_k1[i], lambda_q2[i], lambda_k2[i]]).astype(F32)
        attn = diff_attention(qkv, bias, rel_bias, lamv, subln_gain[i],
                              batch=batch, seq=seq, lam_init=lam_init, t=t_attn)
        y = s5_layer(u, ssm_lambda_re[i], ssm_lambda_im[i], ssm_log_dt[i], ssm_b_re[i], ssm_b_im[i],
                     ssm_c_re[i], ssm_c_im[i], ssm_d[i], batch=batch, seq=seq)
        ssm = glu(y, w_glu[i].astype(BF16), b_glu[i])
        h = outproj(attn, ssm, w_out[i, :attn_width].astype(BF16), w_out[i, attn_width:].astype(BF16), h)
        xn = rmsnorm(h, norm_mlp[i], BF16)
        ff = matmul(xn, w_up[i].astype(BF16), out_dtype=BF16, epilogue="relu2", name="mlp_up")
        h = matmul(ff, w_down[i].astype(BF16), out_dtype=F32, epilogue="residual", res=h,
                   tk=2048, name="mlp_down")
    out = rmsnorm(h, norm_final, x.dtype)
    return out.reshape(batch, seq, d_model)


def kernel(x, w_in, w_out, norm_mix, norm_mlp, w_up, w_down, norm_final, rel_bias, lambda_q1, lambda_k1, lambda_q2, lambda_k2, subln_gain, ssm_lambda_re, ssm_lambda_im, ssm_log_dt, ssm_b_re, ssm_b_im, ssm_c_re, ssm_c_im, ssm_d, w_glu, b_glu):
    return _forward(x, w_in, w_out, norm_mix, norm_mlp, w_up, w_down, norm_final, rel_bias,
                    lambda_q1, lambda_k1, lambda_q2, lambda_k2, subln_gain,
                    ssm_lambda_re, ssm_lambda_im, ssm_log_dt, ssm_b_re, ssm_b_im,
                    ssm_c_re, ssm_c_im, ssm_d, w_glu, b_glu)
```

```python
import functools
import math

import numpy as np
import jax
import jax.numpy as jnp
from jax import lax
from jax.experimental import pallas as pl
from jax.experimental.pallas import tpu as pltpu

F32 = jnp.float32
BF16 = jnp.bfloat16

EPS = 1e-6
MAX_EXACT = 16
MAX_DISTANCE = 128
DIFF_HEAD_DIM = 64
SSM_GROUP = 16
SSM_CHUNK = 32
NEG_BIG = -1e30
LOG2E = math.log2(math.e)

V7X_VMEM_LIMIT_BYTES = 56 * 1024 * 1024
LANES = 128


def _cparams(*sem):
    return pltpu.CompilerParams(dimension_semantics=sem, vmem_limit_bytes=V7X_VMEM_LIMIT_BYTES)


def _tile(dim, pref):
    t = min(dim, pref)
    while dim % t:
        t //= 2
    return t


def _rmsnorm_kernel(x_ref, g_ref, o_ref):
    x = x_ref[...].astype(F32)
    ms = jnp.mean(x * x, axis=-1, keepdims=True)
    o_ref[...] = ((x * lax.rsqrt(ms + EPS)) * g_ref[...]).astype(o_ref.dtype)


def rmsnorm(x, g, out_dtype):
    m, d = x.shape
    tm = _tile(m, 256)
    return pl.pallas_call(
        _rmsnorm_kernel,
        grid=(m // tm,),
        in_specs=[pl.BlockSpec((tm, d), lambda i: (i, 0)),
                  pl.BlockSpec((1, d), lambda i: (0, 0))],
        out_specs=pl.BlockSpec((tm, d), lambda i: (i, 0)),
        out_shape=jax.ShapeDtypeStruct((m, d), out_dtype),
        compiler_params=_cparams("parallel"),
        name="rmsnorm",
    )(x, g.reshape(1, d).astype(F32))


def _mm_kernel(a_ref, b_ref, *rest, nk, epilogue):
    if epilogue == "residual":
        r_ref, o_ref = rest[0], rest[1]
        rest = rest[2:]
    else:
        r_ref, o_ref = None, rest[0]
        rest = rest[1:]

    def finish(acc):
        if epilogue == "relu2":
            acc = jnp.square(jnp.maximum(acc, 0.0))
        elif epilogue == "residual":
            acc = r_ref[...] + acc
        o_ref[...] = acc.astype(o_ref.dtype)

    part = jnp.dot(a_ref[...], b_ref[...].astype(a_ref.dtype), preferred_element_type=F32)
    if nk == 1:
        finish(part)
        return
    acc_ref = rest[0]
    k = pl.program_id(2)

    @pl.when(k == 0)
    def _():
        acc_ref[...] = part

    @pl.when(jnp.logical_and(k > 0, k < nk - 1))
    def _():
        acc_ref[...] += part

    @pl.when(k == nk - 1)
    def _():
        finish(acc_ref[...] + part)


def matmul(a, w, layer, *, out_dtype, epilogue="none", res=None, tm=2048, tn=1024, tk=1024, name="matmul"):
    m, kdim = a.shape
    _, _, n = w.shape
    tm, tn, tk = _tile(m, tm), _tile(n, tn), _tile(kdim, tk)
    nk = kdim // tk
    in_specs = [pl.BlockSpec((tm, tk), lambda i, j, k: (i, k)),
                pl.BlockSpec((None, tk, tn), lambda i, j, k: (layer, k, j))]
    args = [a, w]
    if epilogue == "residual":
        in_specs.append(pl.BlockSpec((tm, tn), lambda i, j, k: (i, j)))
        args.append(res)
    scratch = [pltpu.VMEM((tm, tn), F32)] if nk > 1 else []
    return pl.pallas_call(
        functools.partial(_mm_kernel, nk=nk, epilogue=epilogue),
        grid=(m // tm, n // tn, nk),
        in_specs=in_specs,
        out_specs=pl.BlockSpec((tm, tn), lambda i, j, k: (i, j)),
        out_shape=jax.ShapeDtypeStruct((m, n), out_dtype),
        scratch_shapes=scratch,
        compiler_params=_cparams("parallel", "parallel", "arbitrary"),
        name=name,
    )(*args)


def _outproj_kernel(a1_ref, a2_ref, b1_ref, b2_ref, r_ref, o_ref):
    acc = jnp.dot(a1_ref[...], b1_ref[...].astype(a1_ref.dtype), preferred_element_type=F32)
    acc = acc + jnp.dot(a2_ref[...], b2_ref[...].astype(a2_ref.dtype), preferred_element_type=F32)
    o_ref[...] = r_ref[...] + acc


def outproj(a1, a2, w, layer, res, *, tm=1024, tn=512):
    m, k1 = a1.shape
    _, k2 = a2.shape
    _, _, n = w.shape
    assert k1 == k2
    tm, tn = _tile(m, tm), _tile(n, tn)
    return pl.pallas_call(
        _outproj_kernel,
        grid=(m // tm, n // tn),
        in_specs=[pl.BlockSpec((tm, k1), lambda i, j: (i, 0)),
                  pl.BlockSpec((tm, k2), lambda i, j: (i, 0)),
                  pl.BlockSpec((None, k1, tn), lambda i, j: (layer, 0, j)),
                  pl.BlockSpec((None, k2, tn), lambda i, j: (layer, 1, j)),
                  pl.BlockSpec((tm, tn), lambda i, j: (i, j))],
        out_specs=pl.BlockSpec((tm, tn), lambda i, j: (i, j)),
        out_shape=jax.ShapeDtypeStruct((m, n), F32),
        compiler_params=_cparams("parallel", "parallel"),
        name="outproj",
    )(a1, a2, w, w, res)


def _glu_kernel(y_ref, w_ref, b_ref, o_ref):
    y = y_ref[...]
    z = jnp.dot(y, w_ref[...], preferred_element_type=F32) + b_ref[...]
    o_ref[...] = (y.astype(F32) * jax.nn.sigmoid(z)).astype(o_ref.dtype)


def glu(y, w, b, *, tm=512):
    m, n = y.shape
    tm = _tile(m, tm)
    return pl.pallas_call(
        _glu_kernel,
        grid=(m // tm,),
        in_specs=[pl.BlockSpec((tm, n), lambda i: (i, 0)),
                  pl.BlockSpec((n, n), lambda i: (0, 0)),
                  pl.BlockSpec((1, n), lambda i: (0, 0))],
        out_specs=pl.BlockSpec((tm, n), lambda i: (i, 0)),
        out_shape=jax.ShapeDtypeStruct((m, n), BF16),
        compiler_params=_cparams("parallel"),
        name="glu",
    )(y, w, b.reshape(1, n).astype(F32))


def _t5_bucket_np(n, num_buckets):
    n = np.maximum(n, 0)
    nf = np.maximum(n, 1).astype(np.float32)
    large = MAX_EXACT + (np.log(nf / np.float32(MAX_EXACT)) / np.float32(math.log(MAX_DISTANCE / MAX_EXACT))
                         * np.float32(num_buckets - MAX_EXACT)).astype(np.int32)
    large = np.minimum(large, num_buckets - 1)
    return np.where(n < MAX_EXACT, n, large).astype(np.int32)


def _bias_tiles_kernel(rb_ref, idx_ref, o_ref, *, num_buckets):
    h = pl.program_id(0)
    for d in range(idx_ref.shape[0]):
        idx = idx_ref[d]
        acc = jnp.full(idx.shape, NEG_BIG, F32)
        for k in range(num_buckets):
            acc = jnp.where(idx == k, rb_ref[k, h] * LOG2E, acc)
        o_ref[0, d] = acc


def bias_tiles(rel_bias, t, seq):
    num_buckets, n_heads = rel_bias.shape
    c = np.arange(t)[:, None]
    r = np.arange(t)[None, :]
    far = _t5_bucket_np(np.arange(t + 1, max(seq, t + 2)), num_buckets)
    assert np.all(far == far[0]), "kv tiles two or more steps left of the diagonal must share one bucket"
    idx_far = np.full((t, t), far[0])
    idx_left = _t5_bucket_np(t + r - c, num_buckets)
    idx_diag = np.where(r - c >= 0, _t5_bucket_np(r - c, num_buckets), -1)
    idx = jnp.asarray(np.stack([idx_far, idx_left, idx_diag]).astype(np.int32))
    return pl.pallas_call(
        functools.partial(_bias_tiles_kernel, num_buckets=num_buckets),
        grid=(n_heads,),
        in_specs=[pl.BlockSpec(memory_space=pltpu.SMEM),
                  pl.BlockSpec((3, t, t), lambda h: (0, 0, 0))],
        out_specs=pl.BlockSpec((1, 3, t, t), lambda h: (h, 0, 0, 0)),
        out_shape=jax.ShapeDtypeStruct((n_heads, 3, t, t), F32),
        compiler_params=_cparams("arbitrary"),
        name="bias_tiles",
    )(rel_bias.astype(F32), idx)


def _attn_kernel(lamv_ref, gain_ref, bias_ref, q_ref, k_ref, v_ref, o_ref,
                 vt_ref, qt_ref, sa_ref, sb_ref, mxa_ref, mxb_ref, m_ref, l_ref, acc_ref, *, t, lam_init):
    qi = pl.program_id(2)
    d = DIFF_HEAD_DIM
    hd = 2 * d
    nkv = vt_ref.shape[0]

    @pl.when(qi == 0)
    def _():
        for c in range(nkv):
            vt_ref[c] = v_ref[c * t:(c + 1) * t, :].astype(F32).T.astype(vt_ref.dtype)

    qT = (q_ref[...].astype(F32) * (d ** -0.5 * LOG2E)).T
    row = lax.broadcasted_iota(jnp.int32, (hd, t), 0)
    qt_ref[:, 0:t] = jnp.where(row < d, qT, 0.0).astype(qt_ref.dtype)
    qt_ref[:, t:2 * t] = jnp.where(row >= d, qT, 0.0).astype(qt_ref.dtype)

    m_ref[...] = jnp.full(m_ref.shape, NEG_BIG, F32)
    l_ref[...] = jnp.zeros(l_ref.shape, F32)
    acc_ref[...] = jnp.zeros(acc_ref.shape, F32)

    buf_a = (sa_ref, mxa_ref)
    buf_b = (sb_ref, mxb_ref)

    def scores(j, buf):
        s_ref, mx_ref = buf
        kblk = k_ref[pl.ds(pl.multiple_of(j * t, t), t), :]
        s = jnp.dot(kblk, qt_ref[...], preferred_element_type=F32)
        b = bias_ref[0, jnp.clip(j - qi + 2, 0, 2)]
        for c in range(2):
            sc = s[:, c * t:(c + 1) * t] + b
            s_ref[:, c * t:(c + 1) * t] = sc
            mx_ref[:, c * t:(c + 1) * t] = jnp.max(sc, axis=0, keepdims=True)

    def reduce(buf):
        s_ref, mx_ref = buf
        m_prev = m_ref[...]
        m_new = jnp.maximum(m_prev, mx_ref[...])
        alpha = jnp.exp2(m_prev - m_new)
        p = jnp.exp2(s_ref[...] - m_new)
        l_ref[...] = alpha * l_ref[...] + jnp.sum(p, axis=0, keepdims=True)
        m_ref[...] = m_new
        return alpha, p.astype(vt_ref.dtype)

    def accumulate(j, alpha, p):
        acc_ref[...] = alpha * acc_ref[...] + jnp.dot(vt_ref[j], p, preferred_element_type=F32)

    def step(j, cur, nxt):
        alpha, p = reduce(cur)
        scores(j + 1, nxt)
        accumulate(j, alpha, p)

    odd = (qi & 1) == 1

    @pl.when(jnp.logical_not(odd))
    def _():
        scores(0, buf_a)

    @pl.when(odd)
    def _():
        scores(0, buf_b)
        step(0, buf_b, buf_a)

    def pair(pp, carry):
        j = (qi & 1) + 2 * pp
        step(j, buf_a, buf_b)
        step(j + 1, buf_b, buf_a)
        return carry

    lax.fori_loop(0, lax.shift_right_logical(qi, 1), pair, 0)
    alpha, p = reduce(buf_a)
    accumulate(qi, alpha, p)

    lv = lamv_ref[...]
    lam = (jnp.exp(jnp.sum(lv[0:1] * lv[1:2], axis=-1, keepdims=True))
           - jnp.exp(jnp.sum(lv[2:3] * lv[3:4], axis=-1, keepdims=True)) + lam_init)
    on = acc_ref[...] * (1.0 / l_ref[...])
    o = on[:, 0:t] - lam * on[:, t:2 * t]
    o = o * lax.rsqrt(jnp.mean(o * o, axis=0, keepdims=True) + EPS)
    o = (o * gain_ref[...]) * (1.0 - lam_init)
    o_ref[...] = o.T.astype(o_ref.dtype)


def diff_attention(qkv, bias, rel_bias, lamv, gain, *, batch, seq, lam_init, t):
    n_heads = rel_bias.shape[1]
    hd = 2 * DIFF_HEAD_DIM
    nq = seq // t
    kernel = functools.partial(_attn_kernel, t=t, lam_init=lam_init)
    return pl.pallas_call(
        kernel,
        grid=(batch, n_heads, nq),
        in_specs=[pl.BlockSpec((4, DIFF_HEAD_DIM), lambda b, h, i: (0, 0)),
                  pl.BlockSpec((hd, 1), lambda b, h, i: (0, 0)),
                  pl.BlockSpec((1, 3, t, t), lambda b, h, i: (h, 0, 0, 0)),
                  pl.BlockSpec((t, hd), lambda b, h, i: (b * nq + i, h)),
                  pl.BlockSpec((seq, hd), lambda b, h, i: (b, n_heads + h)),
                  pl.BlockSpec((seq, hd), lambda b, h, i: (b, 2 * n_heads + h))],
        out_specs=pl.BlockSpec((t, hd), lambda b, h, i: (b * nq + i, h)),
        out_shape=jax.ShapeDtypeStruct((batch * seq, n_heads * hd), BF16),
        scratch_shapes=[pltpu.VMEM((nq, hd, t), BF16), pltpu.VMEM((hd, 2 * t), BF16),
                        pltpu.VMEM((t, 2 * t), F32), pltpu.VMEM((t, 2 * t), F32),
                        pltpu.VMEM((1, 2 * t), F32), pltpu.VMEM((1, 2 * t), F32),
                        pltpu.VMEM((1, 2 * t), F32), pltpu.VMEM((1, 2 * t), F32),
                        pltpu.VMEM((hd, 2 * t), F32)],
        compiler_params=_cparams("parallel", "parallel", "arbitrary"),
        name="diff_attention",
    )(lamv, gain.reshape(hd, 1).astype(F32), bias, qkv, qkv, qkv)


def _s5_prep_kernel(lre_ref, lim_ref, ldt_ref, btr_ref, bti_ref, cr_ref, ci_ref,
                    tg_ref, wg_ref, vgt_ref, ap_ref, e_ref, q_ref, *, gb, chunk, nlev):
    L = chunk
    P = lre_ref.shape[-1]
    W = SSM_GROUP * L
    rows = e_ref.shape[1]
    lane = lax.broadcasted_iota(jnp.int32, (1, 2 * P), 1)
    sgn = jnp.where(lane < P, -1.0, 1.0).astype(F32)
    jj = lax.broadcasted_iota(jnp.int32, (rows, 2 * P), 0).astype(F32)

    def dup(x):
        return jnp.concatenate([x, x], axis=-1)

    def group(g, carry):
        dt = jnp.exp(ldt_ref[g])
        lr, li = lre_ref[g], lim_ref[g]
        lrdt, lidt = lr * dt, li * dt
        mag = jnp.exp(lrdt)
        ar, ai = mag * jnp.cos(lidt), mag * jnp.sin(lidt)
        den = lr * lr + li * li
        nr, ni = ar - 1.0, ai
        gr = (nr * lr + ni * li) / den
        gi = (ni * lr - nr * li) / den
        btr, bti = btr_ref[g], bti_ref[g]
        bbr = gr * btr - gi * bti
        bbi = gr * bti + gi * btr
        cr, ci = cr_ref[g], ci_ref[g]
        bc1 = jnp.concatenate([bbr, bbi], axis=-1)
        bc2 = jnp.concatenate([-bbi, bbr], axis=-1)
        cc1 = jnp.concatenate([cr, -ci], axis=-1)
        cc2 = jnp.concatenate([-ci, -cr], axis=-1)

        mg = jnp.exp(jj * dup(lrdt))
        ang = jj * dup(lidt)
        e_ref[0] = mg * jnp.cos(ang)
        e_ref[1] = mg * jnp.sin(ang)

        for j in range(L + 1):
            er = jnp.broadcast_to(e_ref[0, j:j + 1, :], (SSM_GROUP, 2 * P))
            ei = jnp.broadcast_to(e_ref[1, j:j + 1, :], (SSM_GROUP, 2 * P))
            q_ref[j * SSM_GROUP:(j + 1) * SSM_GROUP, :] = er * cc1 + ei * cc2
            if j < L:
                tau = L - 1 - j
                wg_ref[g, tau * SSM_GROUP:(tau + 1) * SSM_GROUP, :] = (er * bc1 + ei * bc2).astype(wg_ref.dtype)

        vgt_ref[g] = q_ref[SSM_GROUP:SSM_GROUP * (L + 1), :].astype(vgt_ref.dtype)
        strip = lax.dot_general(bc1, q_ref[0:W, :], (((1,), (1,)), ((), ())),
                                precision=lax.Precision.HIGHEST, preferred_element_type=F32)
        pad = jnp.concatenate([jnp.zeros_like(strip), strip], axis=-1)
        lanes_per_step = LANES // SSM_GROUP
        for r in range(lanes_per_step):
            rolled = pad if r == 0 else pltpu.roll(pad, SSM_GROUP * r, axis=1)
            for qd in range(L // lanes_per_step):
                tau = lanes_per_step * qd + r
                tg_ref[g, tau * SSM_GROUP:(tau + 1) * SSM_GROUP, :] = (
                    rolled[:, W - LANES * qd:2 * W - LANES * qd].astype(tg_ref.dtype))

        pr, pi = e_ref[0, L:L + 1, :], e_ref[1, L:L + 1, :]
        for k in range(nlev):
            ap_ref[g, k, 0:1, :] = pr
            ap_ref[g, k, 1:2, :] = pi * sgn
            pr, pi = pr * pr - pi * pi, 2.0 * pr * pi
        return carry

    lax.fori_loop(0, gb, group, 0)


def s5_prep(lam_re, lam_im, log_dt, bt_re, bt_im, c_re, c_im, *, chunk, nlev, gb=8):
    G, P = lam_re.shape
    H = SSM_GROUP
    W = H * chunk
    assert 2 * P == LANES and chunk % (LANES // H) == 0
    gb = _tile(G, gb)
    rows = -(-(chunk + 1) // 8) * 8
    vec = lambda a: a.reshape(G, 1, -1).astype(F32)
    g3 = lambda s1, s2: pl.BlockSpec((gb, s1, s2), lambda i: (i, 0, 0))
    return pl.pallas_call(
        functools.partial(_s5_prep_kernel, gb=gb, chunk=chunk, nlev=nlev),
        grid=(G // gb,),
        in_specs=[g3(1, P), g3(1, P), g3(1, 1), g3(H, P), g3(H, P), g3(H, P), g3(H, P)],
        out_specs=[g3(W, W), g3(W, 2 * P), g3(W, 2 * P),
                   pl.BlockSpec((gb, nlev, 2, 2 * P), lambda i: (i, 0, 0, 0))],
        out_shape=[jax.ShapeDtypeStruct((G, W, W), BF16),
                   jax.ShapeDtypeStruct((G, W, 2 * P), BF16),
                   jax.ShapeDtypeStruct((G, W, 2 * P), BF16),
                   jax.ShapeDtypeStruct((G, nlev, 2, 2 * P), F32)],
        scratch_shapes=[pltpu.VMEM((2, rows, 2 * P), F32),
                        pltpu.VMEM((H * (chunk + 1), 2 * P), F32)],
        compiler_params=_cparams("parallel"),
        name="s5_prep",
    )(vec(lam_re), vec(lam_im), vec(log_dt), bt_re.astype(F32), bt_im.astype(F32),
      c_re.astype(F32), c_im.astype(F32))


def _s5_kernel(x_ref, tg_ref, wg_ref, vgt_ref, ap_ref, d_ref, o_ref, *, gb, nc, nlev):
    n = x_ref.shape[1]
    P2 = wg_ref.shape[-1]
    cidx = lax.broadcasted_iota(jnp.int32, (n, P2), 0) % nc

    def shift_rows(s, sh):
        return jnp.where(cidx >= sh, pltpu.roll(s, sh, axis=0), 0.0)

    def group(g, carry):
        xb = x_ref[g]
        s = jnp.dot(xb, wg_ref[g], preferred_element_type=F32)
        for k in range(nlev):
            sh = shift_rows(s, 1 << k)
            s = s + sh * ap_ref[g, k, 0:1, :] + pltpu.roll(sh, P2 // 2, axis=1) * ap_ref[g, k, 1:2, :]
        s_in = shift_rows(s, 1)
        y = jnp.dot(xb, tg_ref[g], preferred_element_type=F32)
        y = y + lax.dot_general(s_in.astype(BF16), vgt_ref[g], (((1,), (1,)), ((), ())),
                                preferred_element_type=F32)
        y = y + xb.astype(F32) * d_ref[g]
        o_ref[g] = jax.nn.gelu(y).astype(o_ref.dtype)
        return carry

    lax.fori_loop(0, gb, group, 0)


def s5_apply(xg, tg, wg, vgt, ap, dt, *, nc, nlev, gb=8):
    G, n, W = xg.shape
    gb = _tile(G, gb)
    g3 = lambda s1, s2: pl.BlockSpec((gb, s1, s2), lambda i: (i, 0, 0))
    return pl.pallas_call(
        functools.partial(_s5_kernel, gb=gb, nc=nc, nlev=nlev),
        grid=(G // gb,),
        in_specs=[g3(n, W), g3(W, W), g3(W, LANES), g3(W, LANES),
                  pl.BlockSpec((gb, nlev, 2, LANES), lambda i: (i, 0, 0, 0)), g3(1, W)],
        out_specs=g3(n, W),
        out_shape=jax.ShapeDtypeStruct((G, n, W), BF16),
        compiler_params=_cparams("parallel"),
        name="s5_apply",
    )(xg, tg, wg, vgt, ap, dt)


def s5_layer(u, lam_re, lam_im, log_dt, b_re, b_im, c_re, c_im, d, *, batch, seq):
    T, width = u.shape
    G = width // SSM_GROUP
    L = _tile(seq, SSM_CHUNK)
    nc = seq // L
    nlev = max(nc - 1, 0).bit_length()
    n = batch * nc
    W = SSM_GROUP * L
    bt_re = jnp.swapaxes(b_re, -1, -2)
    bt_im = jnp.swapaxes(b_im, -1, -2)
    tg, wg, vgt, ap = s5_prep(lam_re, lam_im, log_dt, bt_re, bt_im, c_re, c_im, chunk=L, nlev=max(nlev, 1))
    xg = u.reshape(n, L, G, SSM_GROUP).transpose(2, 0, 1, 3).reshape(G, n, W)
    dtile = jnp.tile(d.astype(F32), (1, L)).reshape(G, 1, W)
    yg = s5_apply(xg, tg, wg, vgt, ap, dtile, nc=nc, nlev=nlev)
    return yg.reshape(G, n, L, SSM_GROUP).transpose(1, 2, 0, 3).reshape(T, width)


@jax.jit
def _forward(x, w_in, w_out, norm_mix, norm_mlp, w_up, w_down, norm_final, rel_bias,
             lambda_q1, lambda_k1, lambda_q2, lambda_k2, subln_gain,
             ssm_lambda_re, ssm_lambda_im, ssm_log_dt, ssm_b_re, ssm_b_im,
             ssm_c_re, ssm_c_im, ssm_d, w_glu, b_glu):
    batch, seq, d_model = x.shape
    depth = w_in.shape[0]
    ssm_width = w_glu.shape[-1]
    attn_width = (w_in.shape[-1] - ssm_width) // 3
    T = batch * seq

    t_attn = _tile(seq, 256)
    bias = bias_tiles(rel_bias, t_attn, seq)

    h = x.reshape(T, d_model).astype(F32)
    for i in range(depth):
        lam_init = 0.8 - 0.6 * math.exp(-0.3 * i)
        xn = rmsnorm(h, norm_mix[i], BF16)
        proj = matmul(xn, w_in, i, out_dtype=BF16, name="proj_in")
        lamv = jnp.stack([lambda_q1[i], lambda_k1[i], lambda_q2[i], lambda_k2[i]]).astype(F32)
        attn = diff_attention(proj, bias, rel_bias, lamv, subln_gain[i],
                              batch=batch, seq=seq, lam_init=lam_init, t=t_attn)
        y = s5_layer(proj[:, 3 * attn_width:], ssm_lambda_re[i], ssm_lambda_im[i], ssm_log_dt[i],
                     ssm_b_re[i], ssm_b_im[i], ssm_c_re[i], ssm_c_im[i], ssm_d[i], batch=batch, seq=seq)
        ssm = glu(y, w_glu[i].astype(BF16), b_glu[i])
        h = outproj(attn, ssm, w_out, i, h)
        xn = rmsnorm(h, norm_mlp[i], BF16)
        ff = matmul(xn, w_up, i, out_dtype=BF16, epilogue="relu2", name="mlp_up")
        h = matmul(ff, w_down, i, out_dtype=F32, epilogue="residual", res=h, tk=512, name="mlp_down")
    out = rmsnorm(h, norm_final, x.dtype)
    return out.reshape(batch, seq, d_model)


def kernel(x, w_in, w_out, norm_mix, norm_mlp, w_up, w_down, norm_final, rel_bias, lambda_q1, lambda_k1, lambda_q2, lambda_k2, subln_gain, ssm_lambda_re, ssm_lambda_im, ssm_log_dt, ssm_b_re, ssm_b_im, ssm_c_re, ssm_c_im, ssm_d, w_glu, b_glu):
    return _forward(x, w_in, w_out, norm_mix, norm_mlp, w_up, w_down, norm_final, rel_bias,
                    lambda_q1, lambda_k1, lambda_q2, lambda_k2, subln_gain,
                    ssm_lambda_re, ssm_lambda_im, ssm_log_dt, ssm_b_re, ssm_b_im,
                    ssm_c_re, ssm_c_im, ssm_d, w_glu, b_glu)
```

```python
import functools
import math

import numpy as np
import jax
import jax.numpy as jnp
from jax import lax
from jax.experimental import pallas as pl
from jax.experimental.pallas import tpu as pltpu

F32 = jnp.float32
BF16 = jnp.bfloat16

EPS = 1e-6
MAX_EXACT = 16
MAX_DISTANCE = 128
DIFF_HEAD_DIM = 64
SSM_GROUP = 16
SSM_CHUNK = 32
NEG_BIG = -1e30
LOG2E = math.log2(math.e)

V7X_VMEM_LIMIT_BYTES = 56 * 1024 * 1024
LANES = 128


def _cparams(*sem):
    return pltpu.CompilerParams(dimension_semantics=sem, vmem_limit_bytes=V7X_VMEM_LIMIT_BYTES)


def _tile(dim, pref):
    t = min(dim, pref)
    while dim % t:
        t //= 2
    return t


def _rmsnorm_kernel(x_ref, g_ref, o_ref):
    x = x_ref[...].astype(F32)
    ms = jnp.mean(x * x, axis=-1, keepdims=True)
    o_ref[...] = ((x * lax.rsqrt(ms + EPS)) * g_ref[...]).astype(o_ref.dtype)


def rmsnorm(x, g, out_dtype):
    m, d = x.shape
    tm = _tile(m, 256)
    return pl.pallas_call(
        _rmsnorm_kernel,
        grid=(m // tm,),
        in_specs=[pl.BlockSpec((tm, d), lambda i: (i, 0)),
                  pl.BlockSpec((1, d), lambda i: (0, 0))],
        out_specs=pl.BlockSpec((tm, d), lambda i: (i, 0)),
        out_shape=jax.ShapeDtypeStruct((m, d), out_dtype),
        compiler_params=_cparams("parallel"),
        name="rmsnorm",
    )(x, g.reshape(1, d).astype(F32))


def _mm_kernel(a_ref, b_ref, *rest, nk, epilogue):
    if epilogue == "residual":
        r_ref, o_ref = rest
    else:
        r_ref, (o_ref,) = None, rest

    if nk == 1:
        acc = jnp.dot(a_ref[...], b_ref[...], preferred_element_type=F32)
        if epilogue == "relu2":
            acc = jnp.square(jnp.maximum(acc, 0.0))
        elif epilogue == "residual":
            acc = r_ref[...] + acc
        o_ref[...] = acc.astype(o_ref.dtype)
        return

    @pl.when(pl.program_id(2) == 0)
    def _():
        o_ref[...] = r_ref[...]

    o_ref[...] += jnp.dot(a_ref[...], b_ref[...], preferred_element_type=F32)


def matmul(a, w, layer, *, out_dtype, epilogue="none", res=None, tm=1024, tn=1024, tk=4096, name="matmul"):
    m, kdim = a.shape
    _, _, n = w.shape
    tm, tn, tk = _tile(m, tm), _tile(n, tn), _tile(kdim, tk)
    nk = kdim // tk
    in_specs = [pl.BlockSpec((tm, tk), lambda i, j, k: (i, k)),
                pl.BlockSpec((None, tk, tn), lambda i, j, k: (layer, k, j))]
    args = [a, w]
    if epilogue == "residual":
        in_specs.append(pl.BlockSpec((tm, tn), lambda i, j, k: (i, j)))
        args.append(res)
    assert nk == 1 or (epilogue == "residual" and out_dtype == F32)
    return pl.pallas_call(
        functools.partial(_mm_kernel, nk=nk, epilogue=epilogue),
        grid=(m // tm, n // tn, nk),
        in_specs=in_specs,
        out_specs=pl.BlockSpec((tm, tn), lambda i, j, k: (i, j)),
        out_shape=jax.ShapeDtypeStruct((m, n), out_dtype),
        compiler_params=_cparams("parallel", "parallel", "arbitrary"),
        name=name,
    )(*args)


def _outproj_kernel(a1_ref, a2_ref, b1_ref, b2_ref, r_ref, o_ref):
    acc = jnp.dot(a1_ref[...], b1_ref[...], preferred_element_type=F32)
    acc = acc + jnp.dot(a2_ref[...], b2_ref[...], preferred_element_type=F32)
    o_ref[...] = r_ref[...] + acc


def outproj(a1, a2, w, layer, res, *, tm=1024, tn=1024):
    m, k1 = a1.shape
    _, k2 = a2.shape
    _, _, n = w.shape
    assert k1 == k2
    tm, tn = _tile(m, tm), _tile(n, tn)
    return pl.pallas_call(
        _outproj_kernel,
        grid=(m // tm, n // tn),
        in_specs=[pl.BlockSpec((tm, k1), lambda i, j: (i, 0)),
                  pl.BlockSpec((tm, k2), lambda i, j: (i, 0)),
                  pl.BlockSpec((None, k1, tn), lambda i, j: (layer, 0, j)),
                  pl.BlockSpec((None, k2, tn), lambda i, j: (layer, 1, j)),
                  pl.BlockSpec((tm, tn), lambda i, j: (i, j))],
        out_specs=pl.BlockSpec((tm, tn), lambda i, j: (i, j)),
        out_shape=jax.ShapeDtypeStruct((m, n), F32),
        compiler_params=_cparams("parallel", "parallel"),
        name="outproj",
    )(a1, a2, w, w, res)


def _glu_kernel(y_ref, w_ref, b_ref, o_ref):
    y = y_ref[...]
    z = jnp.dot(y, w_ref[...], preferred_element_type=F32) + b_ref[...]
    o_ref[...] = (y.astype(F32) * jax.nn.sigmoid(z)).astype(o_ref.dtype)


def glu(y, w, layer, b, *, tm=512):
    m, n = y.shape
    tm = _tile(m, tm)
    return pl.pallas_call(
        _glu_kernel,
        grid=(m // tm,),
        in_specs=[pl.BlockSpec((tm, n), lambda i: (i, 0)),
                  pl.BlockSpec((None, n, n), lambda i: (layer, 0, 0)),
                  pl.BlockSpec((1, n), lambda i: (0, 0))],
        out_specs=pl.BlockSpec((tm, n), lambda i: (i, 0)),
        out_shape=jax.ShapeDtypeStruct((m, n), BF16),
        compiler_params=_cparams("parallel"),
        name="glu",
    )(y, w, b.reshape(1, n).astype(F32))


def _t5_bucket_np(n, num_buckets):
    n = np.maximum(n, 0)
    nf = np.maximum(n, 1).astype(np.float32)
    large = MAX_EXACT + (np.log(nf / np.float32(MAX_EXACT)) / np.float32(math.log(MAX_DISTANCE / MAX_EXACT))
                         * np.float32(num_buckets - MAX_EXACT)).astype(np.int32)
    large = np.minimum(large, num_buckets - 1)
    return np.where(n < MAX_EXACT, n, large).astype(np.int32)


def _bias_tiles_kernel(rb_ref, idx_ref, o_ref, *, num_buckets):
    h = pl.program_id(0)
    for d in range(idx_ref.shape[0]):
        idx = idx_ref[d]
        acc = jnp.full(idx.shape, NEG_BIG, F32)
        for k in range(num_buckets):
            acc = jnp.where(idx == k, rb_ref[k, h] * LOG2E, acc)
        o_ref[0, d] = acc


def bias_tiles(rel_bias, t, seq):
    num_buckets, n_heads = rel_bias.shape
    c = np.arange(t)[:, None]
    r = np.arange(t)[None, :]
    far = _t5_bucket_np(np.arange(t + 1, max(seq, t + 2)), num_buckets)
    assert np.all(far == far[0]), "kv tiles two or more steps left of the diagonal must share one bucket"
    idx_far = np.full((t, t), far[0])
    idx_left = _t5_bucket_np(t + r - c, num_buckets)
    idx_diag = np.where(r - c >= 0, _t5_bucket_np(r - c, num_buckets), -1)
    idx = jnp.asarray(np.stack([idx_far, idx_left, idx_diag]).astype(np.int32))
    return pl.pallas_call(
        functools.partial(_bias_tiles_kernel, num_buckets=num_buckets),
        grid=(n_heads,),
        in_specs=[pl.BlockSpec(memory_space=pltpu.SMEM),
                  pl.BlockSpec((3, t, t), lambda h: (0, 0, 0))],
        out_specs=pl.BlockSpec((1, 3, t, t), lambda h: (h, 0, 0, 0)),
        out_shape=jax.ShapeDtypeStruct((n_heads, 3, t, t), F32),
        compiler_params=_cparams("arbitrary"),
        name="bias_tiles",
    )(rel_bias.astype(F32), idx)


def _attn_kernel(lamv_ref, gain_ref, bias_ref, q_ref, k_ref, v_ref, o_ref,
                 vt_ref, qt_ref, sa_ref, sb_ref, mxa_ref, mxb_ref, m_ref, l_ref, acc_ref, *, t, lam_init):
    qi = pl.program_id(2)
    d = DIFF_HEAD_DIM
    hd = 2 * d
    nkv = vt_ref.shape[0]

    @pl.when(qi == 0)
    def _():
        for c in range(nkv):
            vt_ref[c] = v_ref[c * t:(c + 1) * t, :].astype(F32).T.astype(vt_ref.dtype)

    qT = (q_ref[...].astype(F32) * (d ** -0.5 * LOG2E)).T
    row = lax.broadcasted_iota(jnp.int32, (hd, t), 0)
    qt_ref[:, 0:t] = jnp.where(row < d, qT, 0.0).astype(qt_ref.dtype)
    qt_ref[:, t:2 * t] = jnp.where(row >= d, qT, 0.0).astype(qt_ref.dtype)

    m_ref[...] = jnp.full(m_ref.shape, NEG_BIG, F32)
    l_ref[...] = jnp.zeros(l_ref.shape, F32)
    acc_ref[...] = jnp.zeros(acc_ref.shape, F32)

    buf_a = (sa_ref, mxa_ref)
    buf_b = (sb_ref, mxb_ref)

    def scores(j, buf):
        s_ref, mx_ref = buf
        kblk = k_ref[pl.ds(pl.multiple_of(j * t, t), t), :]
        s = jnp.dot(kblk, qt_ref[...], preferred_element_type=F32)
        b = bias_ref[0, jnp.clip(j - qi + 2, 0, 2)]
        for c in range(2):
            sc = s[:, c * t:(c + 1) * t] + b
            s_ref[:, c * t:(c + 1) * t] = sc
            mx_ref[:, c * t:(c + 1) * t] = jnp.max(sc, axis=0, keepdims=True)

    def reduce(buf):
        s_ref, mx_ref = buf
        m_prev = m_ref[...]
        m_new = jnp.maximum(m_prev, mx_ref[...])
        alpha = jnp.exp2(m_prev - m_new)
        p = jnp.exp2(s_ref[...] - m_new)
        l_ref[...] = alpha * l_ref[...] + jnp.sum(p, axis=0, keepdims=True)
        m_ref[...] = m_new
        return alpha, p.astype(vt_ref.dtype)

    def accumulate(j, alpha, p):
        acc_ref[...] = alpha * acc_ref[...] + jnp.dot(vt_ref[j], p, preferred_element_type=F32)

    def step(j, cur, nxt):
        alpha, p = reduce(cur)
        scores(j + 1, nxt)
        accumulate(j, alpha, p)

    odd = (qi & 1) == 1

    @pl.when(jnp.logical_not(odd))
    def _():
        scores(0, buf_a)

    @pl.when(odd)
    def _():
        scores(0, buf_b)
        step(0, buf_b, buf_a)

    def pair(pp, carry):
        j = (qi & 1) + 2 * pp
        step(j, buf_a, buf_b)
        step(j + 1, buf_b, buf_a)
        return carry

    lax.fori_loop(0, lax.shift_right_logical(qi, 1), pair, 0)
    alpha, p = reduce(buf_a)
    accumulate(qi, alpha, p)

    lv = lamv_ref[...]
    lam = (jnp.exp(jnp.sum(lv[0:1] * lv[1:2], axis=-1, keepdims=True))
           - jnp.exp(jnp.sum(lv[2:3] * lv[3:4], axis=-1, keepdims=True)) + lam_init)
    on = acc_ref[...] * (1.0 / l_ref[...])
    o = on[:, 0:t] - lam * on[:, t:2 * t]
    o = o * lax.rsqrt(jnp.mean(o * o, axis=0, keepdims=True) + EPS)
    o = (o * gain_ref[...]) * (1.0 - lam_init)
    o_ref[...] = o.T.astype(o_ref.dtype)


def diff_attention(qkv, bias, rel_bias, lamv, gain, *, batch, seq, lam_init, t):
    n_heads = rel_bias.shape[1]
    hd = 2 * DIFF_HEAD_DIM
    nq = seq // t
    kernel = functools.partial(_attn_kernel, t=t, lam_init=lam_init)
    return pl.pallas_call(
        kernel,
        grid=(batch, n_heads, nq),
        in_specs=[pl.BlockSpec((4, DIFF_HEAD_DIM), lambda b, h, i: (0, 0)),
                  pl.BlockSpec((hd, 1), lambda b, h, i: (0, 0)),
                  pl.BlockSpec((1, 3, t, t), lambda b, h, i: (h, 0, 0, 0)),
                  pl.BlockSpec((t, hd), lambda b, h, i: (b * nq + i, h)),
                  pl.BlockSpec((seq, hd), lambda b, h, i: (b, n_heads + h)),
                  pl.BlockSpec((seq, hd), lambda b, h, i: (b, 2 * n_heads + h))],
        out_specs=pl.BlockSpec((t, hd), lambda b, h, i: (b * nq + i, h)),
        out_shape=jax.ShapeDtypeStruct((batch * seq, n_heads * hd), BF16),
        scratch_shapes=[pltpu.VMEM((nq, hd, t), BF16), pltpu.VMEM((hd, 2 * t), BF16),
                        pltpu.VMEM((t, 2 * t), F32), pltpu.VMEM((t, 2 * t), F32),
                        pltpu.VMEM((1, 2 * t), F32), pltpu.VMEM((1, 2 * t), F32),
                        pltpu.VMEM((1, 2 * t), F32), pltpu.VMEM((1, 2 * t), F32),
                        pltpu.VMEM((hd, 2 * t), F32)],
        compiler_params=_cparams("parallel", "parallel", "arbitrary"),
        name="diff_attention",
    )(lamv, gain.reshape(hd, 1).astype(F32), bias, qkv, qkv, qkv)


def _s5_prep_kernel(lre_ref, lim_ref, ldt_ref, btr_ref, bti_ref, cr_ref, ci_ref,
                    tg_ref, wg_ref, vgt_ref, ap_ref, e_ref, q_ref, *, gb, chunk, nlev):
    L = chunk
    P = lre_ref.shape[-1]
    W = SSM_GROUP * L
    rows = e_ref.shape[1]
    lane = lax.broadcasted_iota(jnp.int32, (1, 2 * P), 1)
    sgn = jnp.where(lane < P, -1.0, 1.0).astype(F32)
    jj = lax.broadcasted_iota(jnp.int32, (rows, 2 * P), 0).astype(F32)

    def dup(x):
        return jnp.concatenate([x, x], axis=-1)

    def group(g, carry):
        dt = jnp.exp(ldt_ref[g])
        lr, li = lre_ref[g], lim_ref[g]
        lrdt, lidt = lr * dt, li * dt
        mag = jnp.exp(lrdt)
        ar, ai = mag * jnp.cos(lidt), mag * jnp.sin(lidt)
        den = lr * lr + li * li
        nr, ni = ar - 1.0, ai
        gr = (nr * lr + ni * li) / den
        gi = (ni * lr - nr * li) / den
        btr, bti = btr_ref[g], bti_ref[g]
        bbr = gr * btr - gi * bti
        bbi = gr * bti + gi * btr
        cr, ci = cr_ref[g], ci_ref[g]
        bc1 = jnp.concatenate([bbr, bbi], axis=-1)
        bc2 = jnp.concatenate([-bbi, bbr], axis=-1)
        cc1 = jnp.concatenate([cr, -ci], axis=-1)
        cc2 = jnp.concatenate([-ci, -cr], axis=-1)

        mg = jnp.exp(jj * dup(lrdt))
        ang = jj * dup(lidt)
        e_ref[0] = mg * jnp.cos(ang)
        e_ref[1] = mg * jnp.sin(ang)

        for j in range(L + 1):
            er = jnp.broadcast_to(e_ref[0, j:j + 1, :], (SSM_GROUP, 2 * P))
            ei = jnp.broadcast_to(e_ref[1, j:j + 1, :], (SSM_GROUP, 2 * P))
            q_ref[j * SSM_GROUP:(j + 1) * SSM_GROUP, :] = er * cc1 + ei * cc2
            if j < L:
                tau = L - 1 - j
                wg_ref[g, tau * SSM_GROUP:(tau + 1) * SSM_GROUP, :] = (er * bc1 + ei * bc2).astype(wg_ref.dtype)

        vgt_ref[g] = q_ref[SSM_GROUP:SSM_GROUP * (L + 1), :].astype(vgt_ref.dtype)
        strip = lax.dot_general(bc1, q_ref[0:W, :], (((1,), (1,)), ((), ())),
                                precision=lax.Precision.HIGHEST, preferred_element_type=F32)
        pad = jnp.concatenate([jnp.zeros_like(strip), strip], axis=-1)
        lanes_per_step = LANES // SSM_GROUP
        for r in range(lanes_per_step):
            rolled = pad if r == 0 else pltpu.roll(pad, SSM_GROUP * r, axis=1)
            for qd in range(L // lanes_per_step):
                tau = lanes_per_step * qd + r
                tg_ref[g, tau * SSM_GROUP:(tau + 1) * SSM_GROUP, :] = (
                    rolled[:, W - LANES * qd:2 * W - LANES * qd].astype(tg_ref.dtype))

        pr, pi = e_ref[0, L:L + 1, :], e_ref[1, L:L + 1, :]
        for k in range(nlev):
            ap_ref[g, k, 0:1, :] = pr
            ap_ref[g, k, 1:2, :] = pi * sgn
            pr, pi = pr * pr - pi * pi, 2.0 * pr * pi
        return carry

    lax.fori_loop(0, gb, group, 0)


def s5_prep(lam_re, lam_im, log_dt, bt_re, bt_im, c_re, c_im, *, chunk, nlev, gb=8):
    G, P = lam_re.shape
    H = SSM_GROUP
    W = H * chunk
    assert 2 * P == LANES and chunk % (LANES // H) == 0
    gb = _tile(G, gb)
    rows = -(-(chunk + 1) // 8) * 8
    vec = lambda a: a.reshape(G, 1, -1).astype(F32)
    g3 = lambda s1, s2: pl.BlockSpec((gb, s1, s2), lambda i: (i, 0, 0))
    return pl.pallas_call(
        functools.partial(_s5_prep_kernel, gb=gb, chunk=chunk, nlev=nlev),
        grid=(G // gb,),
        in_specs=[g3(1, P), g3(1, P), g3(1, 1), g3(H, P), g3(H, P), g3(H, P), g3(H, P)],
        out_specs=[g3(W, W), g3(W, 2 * P), g3(W, 2 * P),
                   pl.BlockSpec((gb, nlev, 2, 2 * P), lambda i: (i, 0, 0, 0))],
        out_shape=[jax.ShapeDtypeStruct((G, W, W), BF16),
                   jax.ShapeDtypeStruct((G, W, 2 * P), BF16),
                   jax.ShapeDtypeStruct((G, W, 2 * P), BF16),
                   jax.ShapeDtypeStruct((G, nlev, 2, 2 * P), F32)],
        scratch_shapes=[pltpu.VMEM((2, rows, 2 * P), F32),
                        pltpu.VMEM((H * (chunk + 1), 2 * P), F32)],
        compiler_params=_cparams("parallel"),
        name="s5_prep",
    )(vec(lam_re), vec(lam_im), vec(log_dt), bt_re.astype(F32), bt_im.astype(F32),
      c_re.astype(F32), c_im.astype(F32))


def _s5_kernel(x_ref, tg_ref, wg_ref, vgt_ref, ap_ref, d_ref, o_ref, *, gb, nc, nlev):
    n = x_ref.shape[1]
    P2 = wg_ref.shape[-1]
    cidx = lax.broadcasted_iota(jnp.int32, (n, P2), 0) % nc

    def shift_rows(s, sh):
        return jnp.where(cidx >= sh, pltpu.roll(s, sh, axis=0), 0.0)

    def group(g, carry):
        xb = x_ref[g]
        s = jnp.dot(xb, wg_ref[g], preferred_element_type=F32)
        for k in range(nlev):
            sh = shift_rows(s, 1 << k)
            s = s + sh * ap_ref[g, k, 0:1, :] + pltpu.roll(sh, P2 // 2, axis=1) * ap_ref[g, k, 1:2, :]
        s_in = shift_rows(s, 1)
        y = jnp.dot(xb, tg_ref[g], preferred_element_type=F32)
        y = y + lax.dot_general(s_in.astype(BF16), vgt_ref[g], (((1,), (1,)), ((), ())),
                                preferred_element_type=F32)
        y = y + xb.astype(F32) * d_ref[g]
        o_ref[g] = jax.nn.gelu(y).astype(o_ref.dtype)
        return carry

    lax.fori_loop(0, gb, group, 0)


def s5_apply(xg, tg, wg, vgt, ap, dt, *, nc, nlev, gb=8):
    G, n, W = xg.shape
    gb = _tile(G, gb)
    g3 = lambda s1, s2: pl.BlockSpec((gb, s1, s2), lambda i: (i, 0, 0))
    return pl.pallas_call(
        functools.partial(_s5_kernel, gb=gb, nc=nc, nlev=nlev),
        grid=(G // gb,),
        in_specs=[g3(n, W), g3(W, W), g3(W, LANES), g3(W, LANES),
                  pl.BlockSpec((gb, nlev, 2, LANES), lambda i: (i, 0, 0, 0)), g3(1, W)],
        out_specs=g3(n, W),
        out_shape=jax.ShapeDtypeStruct((G, n, W), BF16),
        compiler_params=_cparams("parallel"),
        name="s5_apply",
    )(xg, tg, wg, vgt, ap, dt)


def s5_layer(u, lam_re, lam_im, log_dt, b_re, b_im, c_re, c_im, d, *, batch, seq):
    T, width = u.shape
    G = width // SSM_GROUP
    L = _tile(seq, SSM_CHUNK)
    nc = seq // L
    nlev = max(nc - 1, 0).bit_length()
    n = batch * nc
    W = SSM_GROUP * L
    bt_re = jnp.swapaxes(b_re, -1, -2)
    bt_im = jnp.swapaxes(b_im, -1, -2)
    tg, wg, vgt, ap = s5_prep(lam_re, lam_im, log_dt, bt_re, bt_im, c_re, c_im, chunk=L, nlev=max(nlev, 1))
    xg = u.reshape(n, L, G, SSM_GROUP).transpose(2, 0, 1, 3).reshape(G, n, W)
    dtile = jnp.tile(d.astype(F32), (1, L)).reshape(G, 1, W)
    yg = s5_apply(xg, tg, wg, vgt, ap, dtile, nc=nc, nlev=nlev)
    return yg.reshape(G, n, L, SSM_GROUP).transpose(1, 2, 0, 3).reshape(T, width)


@jax.jit
def _forward(x, w_in, w_out, norm_mix, norm_mlp, w_up, w_down, norm_final, rel_bias,
             lambda_q1, lambda_k1, lambda_q2, lambda_k2, subln_gain,
             ssm_lambda_re, ssm_lambda_im, ssm_log_dt, ssm_b_re, ssm_b_im,
             ssm_c_re, ssm_c_im, ssm_d, w_glu, b_glu):
    batch, seq, d_model = x.shape
    depth = w_in.shape[0]
    ssm_width = w_glu.shape[-1]
    attn_width = (w_in.shape[-1] - ssm_width) // 3
    T = batch * seq

    t_attn = _tile(seq, 512)
    bias = bias_tiles(rel_bias, t_attn, seq)

    w_in, w_out, w_up, w_down, w_glu = (w.astype(BF16) for w in (w_in, w_out, w_up, w_down, w_glu))
    h = x.reshape(T, d_model).astype(F32)
    for i in range(depth):
        lam_init = 0.8 - 0.6 * math.exp(-0.3 * i)
        xn = rmsnorm(h, norm_mix[i], BF16)
        proj = matmul(xn, w_in, i, out_dtype=BF16, name="proj_in")
        lamv = jnp.stack([lambda_q1[i], lambda_k1[i], lambda_q2[i], lambda_k2[i]]).astype(F32)
        attn = diff_attention(proj, bias, rel_bias, lamv, subln_gain[i],
                              batch=batch, seq=seq, lam_init=lam_init, t=t_attn)
        y = s5_layer(proj[:, 3 * attn_width:], ssm_lambda_re[i], ssm_lambda_im[i], ssm_log_dt[i],
                     ssm_b_re[i], ssm_b_im[i], ssm_c_re[i], ssm_c_im[i], ssm_d[i], batch=batch, seq=seq)
        ssm = glu(y, w_glu, i, b_glu[i])
        h = outproj(attn, ssm, w_out, i, h)
        xn = rmsnorm(h, norm_mlp[i], BF16)
        ff = matmul(xn, w_up, i, out_dtype=BF16, epilogue="relu2", name="mlp_up")
        h = matmul(ff, w_down, i, out_dtype=F32, epilogue="residual", res=h, tk=4096, name="mlp_down")
    out = rmsnorm(h, norm_final, x.dtype)
    return out.reshape(batch, seq, d_model)


def kernel(x, w_in, w_out, norm_mix, norm_mlp, w_up, w_down, norm_final, rel_bias, lambda_q1, lambda_k1, lambda_q2, lambda_k2, subln_gain, ssm_lambda_re, ssm_lambda_im, ssm_log_dt, ssm_b_re, ssm_b_im, ssm_c_re, ssm_c_im, ssm_d, w_glu, b_glu):
    return _forward(x, w_in, w_out, norm_mix, norm_mlp, w_up, w_down, norm_final, rel_bias,
                    lambda_q1, lambda_k1, lambda_q2, lambda_k2, subln_gain,
                    ssm_lambda_re, ssm_lambda_im, ssm_log_dt, ssm_b_re, ssm_b_im,
                    ssm_c_re, ssm_c_im, ssm_d, w_glu, b_glu)
```

```python
import functools
import math

import numpy as np
import jax
import jax.numpy as jnp
from jax import lax
from jax.experimental import pallas as pl
from jax.experimental.pallas import tpu as pltpu

F32 = jnp.float32
BF16 = jnp.bfloat16

EPS = 1e-6
MAX_EXACT = 16
MAX_DISTANCE = 128
DIFF_HEAD_DIM = 64
SSM_GROUP = 16
SSM_CHUNK = 32
OCTET = 8
NEG_BIG = -1e30
LOG2E = math.log2(math.e)

V7X_VMEM_LIMIT_BYTES = 56 * 1024 * 1024
LANES = 128


def _cparams(*sem):
    return pltpu.CompilerParams(dimension_semantics=sem, vmem_limit_bytes=V7X_VMEM_LIMIT_BYTES)


def _tile(dim, pref):
    t = min(dim, pref)
    while dim % t:
        t //= 2
    return t


def _rmsnorm_kernel(x_ref, g_ref, o_ref):
    x = x_ref[...].astype(F32)
    ms = jnp.mean(x * x, axis=-1, keepdims=True)
    o_ref[...] = ((x * lax.rsqrt(ms + EPS)) * g_ref[...]).astype(o_ref.dtype)


def rmsnorm(x, g, out_dtype):
    m, d = x.shape
    tm = _tile(m, 256)
    return pl.pallas_call(
        _rmsnorm_kernel,
        grid=(m // tm,),
        in_specs=[pl.BlockSpec((tm, d), lambda i: (i, 0)),
                  pl.BlockSpec((1, d), lambda i: (0, 0))],
        out_specs=pl.BlockSpec((tm, d), lambda i: (i, 0)),
        out_shape=jax.ShapeDtypeStruct((m, d), out_dtype),
        compiler_params=_cparams("parallel"),
        name="rmsnorm",
    )(x, g.reshape(1, d).astype(F32))


def _mm_kernel(a_ref, b_ref, *rest, nk, epilogue):
    if epilogue == "residual":
        r_ref, o_ref = rest
    else:
        r_ref, (o_ref,) = None, rest

    if nk == 1:
        acc = jnp.dot(a_ref[...], b_ref[...], preferred_element_type=F32)
        if epilogue == "relu2":
            acc = jnp.square(jnp.maximum(acc, 0.0))
        elif epilogue == "residual":
            acc = r_ref[...] + acc
        o_ref[...] = acc.astype(o_ref.dtype)
        return

    @pl.when(pl.program_id(2) == 0)
    def _():
        o_ref[...] = r_ref[...]

    o_ref[...] += jnp.dot(a_ref[...], b_ref[...], preferred_element_type=F32)


def matmul(a, w, layer, *, out_dtype, epilogue="none", res=None, tm=1024, tn=1024, tk=4096, name="matmul"):
    m, kdim = a.shape
    _, _, n = w.shape
    tm, tn, tk = _tile(m, tm), _tile(n, tn), _tile(kdim, tk)
    nk = kdim // tk
    in_specs = [pl.BlockSpec((tm, tk), lambda i, j, k: (i, k)),
                pl.BlockSpec((None, tk, tn), lambda i, j, k: (layer, k, j))]
    args = [a, w]
    if epilogue == "residual":
        in_specs.append(pl.BlockSpec((tm, tn), lambda i, j, k: (i, j)))
        args.append(res)
    assert nk == 1 or (epilogue == "residual" and out_dtype == F32)
    return pl.pallas_call(
        functools.partial(_mm_kernel, nk=nk, epilogue=epilogue),
        grid=(m // tm, n // tn, nk),
        in_specs=in_specs,
        out_specs=pl.BlockSpec((tm, tn), lambda i, j, k: (i, j)),
        out_shape=jax.ShapeDtypeStruct((m, n), out_dtype),
        compiler_params=_cparams("parallel", "parallel", "arbitrary"),
        name=name,
    )(*args)


def _mm_wcast_kernel(a_ref, w_ref, o_ref, wb_ref, *, epilogue, rows):
    @pl.when(pl.program_id(1) == 0)
    def _():
        for r in range(0, w_ref.shape[0], rows):
            wb_ref[r:r + rows, :] = w_ref[r:r + rows, :].astype(wb_ref.dtype)

    acc = jnp.dot(a_ref[...], wb_ref[...], preferred_element_type=F32)
    if epilogue == "relu2":
        acc = jnp.square(jnp.maximum(acc, 0.0))
    o_ref[...] = acc.astype(o_ref.dtype)


def matmul_wcast(a, w, layer, *, out_dtype, epilogue="none", tm=1024, tn=512, name="matmul"):
    m, kdim = a.shape
    _, _, n = w.shape
    tm, tn = _tile(m, tm), _tile(n, tn)
    return pl.pallas_call(
        functools.partial(_mm_wcast_kernel, epilogue=epilogue, rows=_tile(kdim, 512)),
        grid=(n // tn, m // tm),
        in_specs=[pl.BlockSpec((tm, kdim), lambda j, i: (i, 0)),
                  pl.BlockSpec((None, kdim, tn), lambda j, i: (layer, 0, j))],
        out_specs=pl.BlockSpec((tm, tn), lambda j, i: (i, j)),
        out_shape=jax.ShapeDtypeStruct((m, n), out_dtype),
        scratch_shapes=[pltpu.VMEM((kdim, tn), a.dtype)],
        compiler_params=_cparams("parallel", "arbitrary"),
        name=name,
    )(a, w)


def _outproj_kernel(a1_ref, a2_ref, b1_ref, b2_ref, r_ref, o_ref):
    acc = jnp.dot(a1_ref[...], b1_ref[...], preferred_element_type=F32)
    acc = acc + jnp.dot(a2_ref[...], b2_ref[...], preferred_element_type=F32)
    o_ref[...] = r_ref[...] + acc


def outproj(a1, a2, w, layer, res, *, tm=1024, tn=1024):
    m, k1 = a1.shape
    _, k2 = a2.shape
    _, _, n = w.shape
    assert k1 == k2
    tm, tn = _tile(m, tm), _tile(n, tn)
    return pl.pallas_call(
        _outproj_kernel,
        grid=(m // tm, n // tn),
        in_specs=[pl.BlockSpec((tm, k1), lambda i, j: (i, 0)),
                  pl.BlockSpec((tm, k2), lambda i, j: (i, 0)),
                  pl.BlockSpec((None, k1, tn), lambda i, j: (layer, 0, j)),
                  pl.BlockSpec((None, k2, tn), lambda i, j: (layer, 1, j)),
                  pl.BlockSpec((tm, tn), lambda i, j: (i, j))],
        out_specs=pl.BlockSpec((tm, tn), lambda i, j: (i, j)),
        out_shape=jax.ShapeDtypeStruct((m, n), F32),
        compiler_params=_cparams("parallel", "parallel"),
        name="outproj",
    )(a1, a2, w, w, res)


def _glu_kernel(y_ref, w_ref, b_ref, o_ref):
    y = y_ref[...]
    z = jnp.dot(y, w_ref[...], preferred_element_type=F32) + b_ref[...]
    o_ref[...] = (y.astype(F32) * jax.nn.sigmoid(z)).astype(o_ref.dtype)


def glu(y, w, layer, b, *, tm=512):
    m, n = y.shape
    tm = _tile(m, tm)
    return pl.pallas_call(
        _glu_kernel,
        grid=(m // tm,),
        in_specs=[pl.BlockSpec((tm, n), lambda i: (i, 0)),
                  pl.BlockSpec((None, n, n), lambda i: (layer, 0, 0)),
                  pl.BlockSpec((1, n), lambda i: (0, 0))],
        out_specs=pl.BlockSpec((tm, n), lambda i: (i, 0)),
        out_shape=jax.ShapeDtypeStruct((m, n), BF16),
        compiler_params=_cparams("parallel"),
        name="glu",
    )(y, w, b.reshape(1, n).astype(F32))


def _t5_bucket_np(n, num_buckets):
    n = np.maximum(n, 0)
    nf = np.maximum(n, 1).astype(np.float32)
    large = MAX_EXACT + (np.log(nf / np.float32(MAX_EXACT)) / np.float32(math.log(MAX_DISTANCE / MAX_EXACT))
                         * np.float32(num_buckets - MAX_EXACT)).astype(np.int32)
    large = np.minimum(large, num_buckets - 1)
    return np.where(n < MAX_EXACT, n, large).astype(np.int32)


def _bias_tiles_kernel(rb_ref, idx_ref, o_ref, *, num_buckets):
    h = pl.program_id(0)
    for d in range(idx_ref.shape[0]):
        idx = idx_ref[d]
        acc = jnp.full(idx.shape, NEG_BIG, F32)
        for k in range(num_buckets):
            acc = jnp.where(idx == k, rb_ref[k, h] * LOG2E, acc)
        o_ref[0, d] = acc


def bias_tiles(rel_bias, t, seq):
    num_buckets, n_heads = rel_bias.shape
    c = np.arange(t)[:, None]
    r = np.arange(t)[None, :]
    far = _t5_bucket_np(np.arange(t + 1, max(seq, t + 2)), num_buckets)
    assert np.all(far == far[0]), "kv tiles two or more steps left of the diagonal must share one bucket"
    idx_far = np.full((t, t), far[0])
    idx_left = _t5_bucket_np(t + r - c, num_buckets)
    idx_diag = np.where(r - c >= 0, _t5_bucket_np(r - c, num_buckets), -1)
    idx = jnp.asarray(np.stack([idx_far, idx_left, idx_diag]).astype(np.int32))
    return pl.pallas_call(
        functools.partial(_bias_tiles_kernel, num_buckets=num_buckets),
        grid=(n_heads,),
        in_specs=[pl.BlockSpec(memory_space=pltpu.SMEM),
                  pl.BlockSpec((3, t, t), lambda h: (0, 0, 0))],
        out_specs=pl.BlockSpec((1, 3, t, t), lambda h: (h, 0, 0, 0)),
        out_shape=jax.ShapeDtypeStruct((n_heads, 3, t, t), F32),
        compiler_params=_cparams("arbitrary"),
        name="bias_tiles",
    )(rel_bias.astype(F32), idx)


def _attn_kernel(lamv_ref, gain_ref, bias_ref, q_ref, k_ref, v_ref, o_ref,
                 vt_ref, qt_ref, sa_ref, sb_ref, mxa_ref, mxb_ref, m_ref, l_ref, acc_ref, *, t, lam_init):
    qi = pl.program_id(2)
    d = DIFF_HEAD_DIM
    hd = 2 * d
    nkv = vt_ref.shape[0]

    @pl.when(qi == 0)
    def _():
        for c in range(nkv):
            vt_ref[c] = v_ref[c * t:(c + 1) * t, :].astype(F32).T.astype(vt_ref.dtype)

    qT = (q_ref[...].astype(F32) * (d ** -0.5 * LOG2E)).T
    row = lax.broadcasted_iota(jnp.int32, (hd, t), 0)
    qt_ref[:, 0:t] = jnp.where(row < d, qT, 0.0).astype(qt_ref.dtype)
    qt_ref[:, t:2 * t] = jnp.where(row >= d, qT, 0.0).astype(qt_ref.dtype)

    m_ref[...] = jnp.full(m_ref.shape, NEG_BIG, F32)
    l_ref[...] = jnp.zeros(l_ref.shape, F32)
    acc_ref[...] = jnp.zeros(acc_ref.shape, F32)

    buf_a = (sa_ref, mxa_ref)
    buf_b = (sb_ref, mxb_ref)

    def scores(j, buf):
        s_ref, mx_ref = buf
        kblk = k_ref[pl.ds(pl.multiple_of(j * t, t), t), :]
        s = jnp.dot(kblk, qt_ref[...], preferred_element_type=F32)
        b = bias_ref[0, jnp.clip(j - qi + 2, 0, 2)]
        for c in range(2):
            sc = s[:, c * t:(c + 1) * t] + b
            s_ref[:, c * t:(c + 1) * t] = sc
            mx_ref[:, c * t:(c + 1) * t] = jnp.max(sc, axis=0, keepdims=True)

    def reduce(buf):
        s_ref, mx_ref = buf
        m_prev = m_ref[...]
        m_new = jnp.maximum(m_prev, mx_ref[...])
        alpha = jnp.exp2(m_prev - m_new)
        p = jnp.exp2(s_ref[...] - m_new)
        l_ref[...] = alpha * l_ref[...] + jnp.sum(p, axis=0, keepdims=True)
        m_ref[...] = m_new
        return alpha, p.astype(vt_ref.dtype)

    def accumulate(j, alpha, p):
        acc_ref[...] = alpha * acc_ref[...] + jnp.dot(vt_ref[j], p, preferred_element_type=F32)

    def step(j, cur, nxt):
        alpha, p = reduce(cur)
        scores(j + 1, nxt)
        accumulate(j, alpha, p)

    odd = (qi & 1) == 1

    @pl.when(jnp.logical_not(odd))
    def _():
        scores(0, buf_a)

    @pl.when(odd)
    def _():
        scores(0, buf_b)
        step(0, buf_b, buf_a)

    def pair(pp, carry):
        j = (qi & 1) + 2 * pp
        step(j, buf_a, buf_b)
        step(j + 1, buf_b, buf_a)
        return carry

    lax.fori_loop(0, lax.shift_right_logical(qi, 1), pair, 0)
    alpha, p = reduce(buf_a)
    accumulate(qi, alpha, p)

    lv = lamv_ref[...]
    lam = (jnp.exp(jnp.sum(lv[0:1] * lv[1:2], axis=-1, keepdims=True))
           - jnp.exp(jnp.sum(lv[2:3] * lv[3:4], axis=-1, keepdims=True)) + lam_init)
    on = acc_ref[...] * (1.0 / l_ref[...])
    o = on[:, 0:t] - lam * on[:, t:2 * t]
    o = o * lax.rsqrt(jnp.mean(o * o, axis=0, keepdims=True) + EPS)
    o = (o * gain_ref[...]) * (1.0 - lam_init)
    o_ref[...] = o.T.astype(o_ref.dtype)


def diff_attention(qkv, bias, rel_bias, lamv, gain, *, batch, seq, lam_init, t):
    n_heads = rel_bias.shape[1]
    hd = 2 * DIFF_HEAD_DIM
    nq = seq // t
    kernel = functools.partial(_attn_kernel, t=t, lam_init=lam_init)
    return pl.pallas_call(
        kernel,
        grid=(batch, n_heads, nq),
        in_specs=[pl.BlockSpec((4, DIFF_HEAD_DIM), lambda b, h, i: (0, 0)),
                  pl.BlockSpec((hd, 1), lambda b, h, i: (0, 0)),
                  pl.BlockSpec((1, 3, t, t), lambda b, h, i: (h, 0, 0, 0)),
                  pl.BlockSpec((t, hd), lambda b, h, i: (b * nq + i, h)),
                  pl.BlockSpec((seq, hd), lambda b, h, i: (b, n_heads + h)),
                  pl.BlockSpec((seq, hd), lambda b, h, i: (b, 2 * n_heads + h))],
        out_specs=pl.BlockSpec((t, hd), lambda b, h, i: (b * nq + i, h)),
        out_shape=jax.ShapeDtypeStruct((batch * seq, n_heads * hd), BF16),
        scratch_shapes=[pltpu.VMEM((nq, hd, t), BF16), pltpu.VMEM((hd, 2 * t), BF16),
                        pltpu.VMEM((t, 2 * t), F32), pltpu.VMEM((t, 2 * t), F32),
                        pltpu.VMEM((1, 2 * t), F32), pltpu.VMEM((1, 2 * t), F32),
                        pltpu.VMEM((1, 2 * t), F32), pltpu.VMEM((1, 2 * t), F32),
                        pltpu.VMEM((hd, 2 * t), F32)],
        compiler_params=_cparams("parallel", "parallel", "arbitrary"),
        name="diff_attention",
    )(lamv, gain.reshape(hd, 1).astype(F32), bias, qkv, qkv, qkv)


def _s5_prep_kernel(lre_ref, lim_ref, ldt_ref, btr_ref, bti_ref, cr_ref, ci_ref,
                    tg_ref, wg_ref, vgt_ref, ap_ref, e_ref, q_ref, *, gb, chunk, nlev):
    L = chunk
    P = lre_ref.shape[-1]
    W = SSM_GROUP * L
    rows = e_ref.shape[1]
    lane = lax.broadcasted_iota(jnp.int32, (1, 2 * P), 1)
    sgn = jnp.where(lane < P, -1.0, 1.0).astype(F32)
    jj = lax.broadcasted_iota(jnp.int32, (rows, 2 * P), 0).astype(F32)

    def dup(x):
        return jnp.concatenate([x, x], axis=-1)

    def group(g, carry):
        dt = jnp.exp(ldt_ref[g])
        lr, li = lre_ref[g], lim_ref[g]
        lrdt, lidt = lr * dt, li * dt
        mag = jnp.exp(lrdt)
        ar, ai = mag * jnp.cos(lidt), mag * jnp.sin(lidt)
        den = lr * lr + li * li
        nr, ni = ar - 1.0, ai
        gr = (nr * lr + ni * li) / den
        gi = (ni * lr - nr * li) / den
        btr, bti = btr_ref[g], bti_ref[g]
        bbr = gr * btr - gi * bti
        bbi = gr * bti + gi * btr
        cr, ci = cr_ref[g], ci_ref[g]
        bc1 = jnp.concatenate([bbr, bbi], axis=-1)
        bc2 = jnp.concatenate([-bbi, bbr], axis=-1)
        cc1 = jnp.concatenate([cr, -ci], axis=-1)
        cc2 = jnp.concatenate([-ci, -cr], axis=-1)

        mg = jnp.exp(jj * dup(lrdt))
        ang = jj * dup(lidt)
        e_ref[0] = mg * jnp.cos(ang)
        e_ref[1] = mg * jnp.sin(ang)

        for j in range(L + 1):
            er = jnp.broadcast_to(e_ref[0, j:j + 1, :], (SSM_GROUP, 2 * P))
            ei = jnp.broadcast_to(e_ref[1, j:j + 1, :], (SSM_GROUP, 2 * P))
            q_ref[j * SSM_GROUP:(j + 1) * SSM_GROUP, :] = er * cc1 + ei * cc2
            if j < L:
                tau = L - 1 - j
                wg_ref[g, tau * SSM_GROUP:(tau + 1) * SSM_GROUP, :] = (er * bc1 + ei * bc2).astype(wg_ref.dtype)

        vgt_ref[g] = q_ref[SSM_GROUP:SSM_GROUP * (L + 1), :].astype(vgt_ref.dtype)
        strip = lax.dot_general(bc1, q_ref[0:W, :], (((1,), (1,)), ((), ())),
                                precision=lax.Precision.HIGHEST, preferred_element_type=F32)
        pad = jnp.concatenate([jnp.zeros_like(strip), strip], axis=-1)
        lanes_per_step = LANES // SSM_GROUP
        for r in range(lanes_per_step):
            rolled = pad if r == 0 else pltpu.roll(pad, SSM_GROUP * r, axis=1)
            for qd in range(L // lanes_per_step):
                tau = lanes_per_step * qd + r
                tg_ref[g, tau * SSM_GROUP:(tau + 1) * SSM_GROUP, :] = (
                    rolled[:, W - LANES * qd:2 * W - LANES * qd].astype(tg_ref.dtype))

        pr, pi = e_ref[0, L:L + 1, :], e_ref[1, L:L + 1, :]
        for k in range(nlev):
            ap_ref[g, k, 0:1, :] = pr
            ap_ref[g, k, 1:2, :] = pi * sgn
            pr, pi = pr * pr - pi * pi, 2.0 * pr * pi
        return carry

    lax.fori_loop(0, gb, group, 0)


def s5_prep(lam_re, lam_im, log_dt, bt_re, bt_im, c_re, c_im, *, chunk, nlev, gb=8):
    G, P = lam_re.shape
    H = SSM_GROUP
    W = H * chunk
    assert 2 * P == LANES and chunk % (LANES // H) == 0
    gb = _tile(G, gb)
    rows = -(-(chunk + 1) // 8) * 8
    vec = lambda a: a.reshape(G, 1, -1).astype(F32)
    g3 = lambda s1, s2: pl.BlockSpec((gb, s1, s2), lambda i: (i, 0, 0))
    return pl.pallas_call(
        functools.partial(_s5_prep_kernel, gb=gb, chunk=chunk, nlev=nlev),
        grid=(G // gb,),
        in_specs=[g3(1, P), g3(1, P), g3(1, 1), g3(H, P), g3(H, P), g3(H, P), g3(H, P)],
        out_specs=[g3(W, W), g3(W, 2 * P), g3(W, 2 * P),
                   pl.BlockSpec((gb, nlev, 2, 2 * P), lambda i: (i, 0, 0, 0))],
        out_shape=[jax.ShapeDtypeStruct((G, W, W), BF16),
                   jax.ShapeDtypeStruct((G, W, 2 * P), BF16),
                   jax.ShapeDtypeStruct((G, W, 2 * P), BF16),
                   jax.ShapeDtypeStruct((G, nlev, 2, 2 * P), F32)],
        scratch_shapes=[pltpu.VMEM((2, rows, 2 * P), F32),
                        pltpu.VMEM((H * (chunk + 1), 2 * P), F32)],
        compiler_params=_cparams("parallel"),
        name="s5_prep",
    )(vec(lam_re), vec(lam_im), vec(log_dt), bt_re.astype(F32), bt_im.astype(F32),
      c_re.astype(F32), c_im.astype(F32))


def _s5_kernel(x_ref, perm_ref, tg_ref, wg_ref, vgt_ref, ap_ref, d_ref, o_ref, xs_ref, ys_ref,
               *, gb, nc, nlev, na):
    n = x_ref.shape[1]
    P2 = wg_ref.shape[-1]
    cidx = lax.broadcasted_iota(jnp.int32, (n, P2), 0) % nc

    def shift_rows(s, sh):
        return jnp.where(cidx >= sh, pltpu.roll(s, sh, axis=0), 0.0)

    for a in range(na):
        seg = jnp.concatenate([x_ref[OCTET * a + b] for b in range(OCTET)], axis=-1)
        seg = jnp.dot(seg, perm_ref[...], preferred_element_type=F32).astype(xs_ref.dtype)
        for g in range(gb):
            xs_ref[g, a] = seg[:, g * LANES:(g + 1) * LANES]

    def group(g, carry):
        xb = jnp.concatenate([xs_ref[g, a] for a in range(na)], axis=-1)
        s = jnp.dot(xb, wg_ref[g], preferred_element_type=F32)
        for k in range(nlev):
            sh = shift_rows(s, 1 << k)
            s = s + sh * ap_ref[g, k, 0:1, :] + pltpu.roll(sh, P2 // 2, axis=1) * ap_ref[g, k, 1:2, :]
        s_in = shift_rows(s, 1)
        y = jnp.dot(xb, tg_ref[g], preferred_element_type=F32)
        y = y + lax.dot_general(s_in.astype(BF16), vgt_ref[g], (((1,), (1,)), ((), ())),
                                preferred_element_type=F32)
        y = y + xb.astype(F32) * d_ref[g]
        yb = jax.nn.gelu(y).astype(ys_ref.dtype)
        for a in range(na):
            ys_ref[a, g] = yb[:, a * LANES:(a + 1) * LANES]
        return carry

    lax.fori_loop(0, gb, group, 0)

    for a in range(na):
        seg = jnp.concatenate([ys_ref[a, g] for g in range(gb)], axis=-1)
        seg = lax.dot_general(seg, perm_ref[...], (((1,), (1,)), ((), ())), preferred_element_type=F32)
        for b in range(OCTET):
            o_ref[OCTET * a + b] = seg[:, b * LANES:(b + 1) * LANES].astype(o_ref.dtype)


def s5_apply(ut, tg, wg, vgt, ap, dt, *, nc, nlev):
    L, n, width = ut.shape
    G = width // SSM_GROUP
    W = SSM_GROUP * L
    gb = LANES // SSM_GROUP
    assert G % gb == 0 and L % OCTET == 0 and gb * SSM_GROUP * OCTET == OCTET * LANES
    lane = np.arange(OCTET * LANES)
    b, g, h = lane // LANES, (lane % LANES) // SSM_GROUP, lane % SSM_GROUP
    perm = np.zeros((OCTET * LANES, OCTET * LANES), np.float32)
    perm[lane, g * (OCTET * SSM_GROUP) + b * SSM_GROUP + h] = 1.0
    g3 = lambda s1, s2: pl.BlockSpec((gb, s1, s2), lambda i: (i, 0, 0))
    io = pl.BlockSpec((L, n, LANES), lambda i: (0, 0, i))
    return pl.pallas_call(
        functools.partial(_s5_kernel, gb=gb, nc=nc, nlev=nlev, na=L // OCTET),
        grid=(G // gb,),
        in_specs=[io, pl.BlockSpec(perm.shape, lambda i: (0, 0)), g3(W, W), g3(W, LANES), g3(W, LANES),
                  pl.BlockSpec((gb, nlev, 2, LANES), lambda i: (i, 0, 0, 0)), g3(1, W)],
        out_specs=io,
        out_shape=jax.ShapeDtypeStruct((L, n, width), BF16),
        scratch_shapes=[pltpu.VMEM((gb, L // OCTET, n, LANES), BF16),
                        pltpu.VMEM((L // OCTET, gb, n, LANES), BF16)],
        compiler_params=_cparams("parallel"),
        name="s5_apply",
    )(ut, jnp.asarray(perm, BF16), tg, wg, vgt, ap, dt)


def s5_layer(u, lam_re, lam_im, log_dt, b_re, b_im, c_re, c_im, d, *, batch, seq):
    T, width = u.shape
    G = width // SSM_GROUP
    L = _tile(seq, SSM_CHUNK)
    nc = seq // L
    nlev = max(nc - 1, 0).bit_length()
    n = batch * nc
    W = SSM_GROUP * L
    bt_re = jnp.swapaxes(b_re, -1, -2)
    bt_im = jnp.swapaxes(b_im, -1, -2)
    tg, wg, vgt, ap = s5_prep(lam_re, lam_im, log_dt, bt_re, bt_im, c_re, c_im, chunk=L, nlev=max(nlev, 1))
    ut = u.reshape(n, L, width).swapaxes(0, 1)
    dtile = jnp.tile(d.astype(F32), (1, L)).reshape(G, 1, W)
    yt = s5_apply(ut, tg, wg, vgt, ap, dtile, nc=nc, nlev=nlev)
    return yt.swapaxes(0, 1).reshape(T, width)


@jax.jit
def _forward(x, w_in, w_out, norm_mix, norm_mlp, w_up, w_down, norm_final, rel_bias,
             lambda_q1, lambda_k1, lambda_q2, lambda_k2, subln_gain,
             ssm_lambda_re, ssm_lambda_im, ssm_log_dt, ssm_b_re, ssm_b_im,
             ssm_c_re, ssm_c_im, ssm_d, w_glu, b_glu):
    batch, seq, d_model = x.shape
    depth = w_in.shape[0]
    ssm_width = w_glu.shape[-1]
    attn_width = (w_in.shape[-1] - ssm_width) // 3
    T = batch * seq

    t_attn = _tile(seq, 512)
    bias = bias_tiles(rel_bias, t_attn, seq)

    w_out, w_down, w_glu = (w.astype(BF16) for w in (w_out, w_down, w_glu))
    h = x.reshape(T, d_model).astype(F32)
    for i in range(depth):
        lam_init = 0.8 - 0.6 * math.exp(-0.3 * i)
        xn = rmsnorm(h, norm_mix[i], BF16)
        proj = matmul_wcast(xn, w_in, i, out_dtype=BF16, name="proj_in")
        lamv = jnp.stack([lambda_q1[i], lambda_k1[i], lambda_q2[i], lambda_k2[i]]).astype(F32)
        attn = diff_attention(proj, bias, rel_bias, lamv, subln_gain[i],
                              batch=batch, seq=seq, lam_init=lam_init, t=t_attn)
        y = s5_layer(proj[:, 3 * attn_width:], ssm_lambda_re[i], ssm_lambda_im[i], ssm_log_dt[i],
                     ssm_b_re[i], ssm_b_im[i], ssm_c_re[i], ssm_c_im[i], ssm_d[i], batch=batch, seq=seq)
        ssm = glu(y, w_glu, i, b_glu[i])
        h = outproj(attn, ssm, w_out, i, h)
        xn = rmsnorm(h, norm_mlp[i], BF16)
        ff = matmul_wcast(xn, w_up, i, out_dtype=BF16, epilogue="relu2", name="mlp_up")
        h = matmul(ff, w_down, i, out_dtype=F32, epilogue="residual", res=h, tk=4096, name="mlp_down")
    out = rmsnorm(h, norm_final, x.dtype)
    return out.reshape(batch, seq, d_model)


def kernel(x, w_in, w_out, norm_mix, norm_mlp, w_up, w_down, norm_final, rel_bias, lambda_q1, lambda_k1, lambda_q2, lambda_k2, subln_gain, ssm_lambda_re, ssm_lambda_im, ssm_log_dt, ssm_b_re, ssm_b_im, ssm_c_re, ssm_c_im, ssm_d, w_glu, b_glu):
    return _forward(x, w_in, w_out, norm_mix, norm_mlp, w_up, w_down, norm_final, rel_bias,
                    lambda_q1, lambda_k1, lambda_q2, lambda_k2, subln_gain,
                    ssm_lambda_re, ssm_lambda_im, ssm_log_dt, ssm_b_re, ssm_b_im,
                    ssm_c_re, ssm_c_im, ssm_d, w_glu, b_glu)
```

```python
import functools
import math

import numpy as np
import jax
import jax.numpy as jnp
from jax import lax
from jax.experimental import pallas as pl
from jax.experimental.pallas import tpu as pltpu

F32 = jnp.float32
BF16 = jnp.bfloat16

EPS = 1e-6
MAX_EXACT = 16
MAX_DISTANCE = 128
DIFF_HEAD_DIM = 64
SSM_GROUP = 16
SSM_CHUNK = 32
OCTET = 8
NEG_BIG = -1e30
LOG2E = math.log2(math.e)

V7X_VMEM_LIMIT_BYTES = 56 * 1024 * 1024
LANES = 128
BF16_SUBLANES = 16


def _cparams(*sem):
    return pltpu.CompilerParams(dimension_semantics=sem, vmem_limit_bytes=V7X_VMEM_LIMIT_BYTES)


def _tile(dim, pref):
    t = min(dim, pref)
    while dim % t:
        t //= 2
    return t


def _rmsnorm_kernel(x_ref, g_ref, o_ref):
    x = x_ref[...].astype(F32)
    ms = jnp.mean(x * x, axis=-1, keepdims=True)
    o_ref[...] = ((x * lax.rsqrt(ms + EPS)) * g_ref[...]).astype(o_ref.dtype)


def rmsnorm(x, g, out_dtype):
    m, d = x.shape
    tm = _tile(m, 256)
    return pl.pallas_call(
        _rmsnorm_kernel,
        grid=(m // tm,),
        in_specs=[pl.BlockSpec((tm, d), lambda i: (i, 0)),
                  pl.BlockSpec((1, d), lambda i: (0, 0))],
        out_specs=pl.BlockSpec((tm, d), lambda i: (i, 0)),
        out_shape=jax.ShapeDtypeStruct((m, d), out_dtype),
        compiler_params=_cparams("parallel"),
        name="rmsnorm",
    )(x, g.reshape(1, d).astype(F32))


def _mm_kernel(a_ref, b_ref, *rest, nk, epilogue, cast):
    rest = list(rest)
    r_ref = rest.pop(0) if epilogue == "residual" else None
    if cast:
        c_ref, o_ref, co_ref = rest
        co_ref[...] = c_ref[...].astype(co_ref.dtype)
    else:
        (o_ref,) = rest

    if nk == 1:
        acc = jnp.dot(a_ref[...], b_ref[...], preferred_element_type=F32)
        if epilogue == "relu2":
            acc = jnp.square(jnp.maximum(acc, 0.0))
        elif epilogue == "residual":
            acc = r_ref[...] + acc
        o_ref[...] = acc.astype(o_ref.dtype)
        return

    @pl.when(pl.program_id(2) == 0)
    def _():
        o_ref[...] = r_ref[...]

    o_ref[...] += jnp.dot(a_ref[...], b_ref[...], preferred_element_type=F32)


def matmul(a, w, *, out_dtype, epilogue="none", res=None, cast=None, tm=1024, tn=1024, tk=4096, name="matmul"):
    m, kdim = a.shape
    _, n = w.shape
    tm, tn, tk = _tile(m, tm), _tile(n, tn), _tile(kdim, tk)
    gm, gn, nk = m // tm, n // tn, kdim // tk
    in_specs = [pl.BlockSpec((tm, tk), lambda i, j, k: (i, k)),
                pl.BlockSpec((tk, tn), lambda i, j, k: (k, j))]
    args = [a, w]
    if epilogue == "residual":
        in_specs.append(pl.BlockSpec((tm, tn), lambda i, j, k: (i, j)))
        args.append(res)
    out_specs = [pl.BlockSpec((tm, tn), lambda i, j, k: (i, j))]
    out_shape = [jax.ShapeDtypeStruct((m, n), out_dtype)]
    if cast is not None:
        w_src, layer = cast
        _, ck, cn = w_src.shape
        rows = ck // (gm * gn * nk)
        assert rows * gm * gn * nk == ck and rows % BF16_SUBLANES == 0
        step = lambda i, j, k: (i * gn + j) * nk + k
        in_specs.append(pl.BlockSpec((None, rows, cn), lambda i, j, k: (layer, step(i, j, k), 0)))
        args.append(w_src)
        out_specs.append(pl.BlockSpec((rows, cn), lambda i, j, k: (step(i, j, k), 0)))
        out_shape.append(jax.ShapeDtypeStruct((ck, cn), a.dtype))
    assert nk == 1 or (epilogue == "residual" and out_dtype == F32)
    outs = pl.pallas_call(
        functools.partial(_mm_kernel, nk=nk, epilogue=epilogue, cast=cast is not None),
        grid=(gm, gn, nk),
        in_specs=in_specs,
        out_specs=out_specs,
        out_shape=out_shape,
        compiler_params=_cparams("parallel", "parallel", "arbitrary"),
        name=name,
    )(*args)
    return outs if cast is not None else outs[0]


def _outproj_kernel(a1_ref, a2_ref, b1_ref, b2_ref, r_ref, o_ref):
    acc = jnp.dot(a1_ref[...], b1_ref[...], preferred_element_type=F32)
    acc = acc + jnp.dot(a2_ref[...], b2_ref[...], preferred_element_type=F32)
    o_ref[...] = r_ref[...] + acc


def outproj(a1, a2, w, layer, res, *, tm=1024, tn=1024):
    m, k1 = a1.shape
    _, k2 = a2.shape
    _, _, n = w.shape
    assert k1 == k2
    tm, tn = _tile(m, tm), _tile(n, tn)
    return pl.pallas_call(
        _outproj_kernel,
        grid=(m // tm, n // tn),
        in_specs=[pl.BlockSpec((tm, k1), lambda i, j: (i, 0)),
                  pl.BlockSpec((tm, k2), lambda i, j: (i, 0)),
                  pl.BlockSpec((None, k1, tn), lambda i, j: (layer, 0, j)),
                  pl.BlockSpec((None, k2, tn), lambda i, j: (layer, 1, j)),
                  pl.BlockSpec((tm, tn), lambda i, j: (i, j))],
        out_specs=pl.BlockSpec((tm, tn), lambda i, j: (i, j)),
        out_shape=jax.ShapeDtypeStruct((m, n), F32),
        compiler_params=_cparams("parallel", "parallel"),
        name="outproj",
    )(a1, a2, w, w, res)


def _glu_kernel(y_ref, w_ref, b_ref, *rest, cast):
    if cast:
        c_ref, o_ref, co_ref = rest
        co_ref[...] = c_ref[...].astype(co_ref.dtype)
    else:
        (o_ref,) = rest
    y = y_ref[...]
    z = jnp.dot(y, w_ref[...], preferred_element_type=F32) + b_ref[...]
    o_ref[...] = (y.astype(F32) * jax.nn.sigmoid(z)).astype(o_ref.dtype)


def glu(y, w, layer, b, *, cast=None, tm=512):
    m, n = y.shape
    tm = _tile(m, tm)
    steps = m // tm
    in_specs = [pl.BlockSpec((tm, n), lambda i: (i, 0)),
                pl.BlockSpec((None, n, n), lambda i: (layer, 0, 0)),
                pl.BlockSpec((1, n), lambda i: (0, 0))]
    args = [y, w, b.reshape(1, n).astype(F32)]
    out_specs = [pl.BlockSpec((tm, n), lambda i: (i, 0))]
    out_shape = [jax.ShapeDtypeStruct((m, n), BF16)]
    if cast is not None:
        w_src, src_layer = cast
        _, ck, cn = w_src.shape
        rows = ck // steps
        assert rows * steps == ck and rows % BF16_SUBLANES == 0
        in_specs.append(pl.BlockSpec((None, rows, cn), lambda i: (src_layer, i, 0)))
        args.append(w_src)
        out_specs.append(pl.BlockSpec((rows, cn), lambda i: (i, 0)))
        out_shape.append(jax.ShapeDtypeStruct((ck, cn), y.dtype))
    outs = pl.pallas_call(
        functools.partial(_glu_kernel, cast=cast is not None),
        grid=(steps,),
        in_specs=in_specs,
        out_specs=out_specs,
        out_shape=out_shape,
        compiler_params=_cparams("parallel"),
        name="glu",
    )(*args)
    return outs if cast is not None else outs[0]


def _t5_bucket_np(n, num_buckets):
    n = np.maximum(n, 0)
    nf = np.maximum(n, 1).astype(np.float32)
    large = MAX_EXACT + (np.log(nf / np.float32(MAX_EXACT)) / np.float32(math.log(MAX_DISTANCE / MAX_EXACT))
                         * np.float32(num_buckets - MAX_EXACT)).astype(np.int32)
    large = np.minimum(large, num_buckets - 1)
    return np.where(n < MAX_EXACT, n, large).astype(np.int32)


def _bias_tiles_kernel(rb_ref, idx_ref, o_ref, *, num_buckets):
    h = pl.program_id(0)
    for d in range(idx_ref.shape[0]):
        idx = idx_ref[d]
        acc = jnp.full(idx.shape, NEG_BIG, F32)
        for k in range(num_buckets):
            acc = jnp.where(idx == k, rb_ref[k, h] * LOG2E, acc)
        o_ref[0, d] = acc


def bias_tiles(rel_bias, t, seq):
    num_buckets, n_heads = rel_bias.shape
    c = np.arange(t)[:, None]
    r = np.arange(t)[None, :]
    far = _t5_bucket_np(np.arange(t + 1, max(seq, t + 2)), num_buckets)
    assert np.all(far == far[0]), "kv tiles two or more steps left of the diagonal must share one bucket"
    idx_far = np.full((t, t), far[0])
    idx_left = _t5_bucket_np(t + r - c, num_buckets)
    idx_diag = np.where(r - c >= 0, _t5_bucket_np(r - c, num_buckets), -1)
    idx = jnp.asarray(np.stack([idx_far, idx_left, idx_diag]).astype(np.int32))
    return pl.pallas_call(
        functools.partial(_bias_tiles_kernel, num_buckets=num_buckets),
        grid=(n_heads,),
        in_specs=[pl.BlockSpec(memory_space=pltpu.SMEM),
                  pl.BlockSpec((3, t, t), lambda h: (0, 0, 0))],
        out_specs=pl.BlockSpec((1, 3, t, t), lambda h: (h, 0, 0, 0)),
        out_shape=jax.ShapeDtypeStruct((n_heads, 3, t, t), F32),
        compiler_params=_cparams("arbitrary"),
        name="bias_tiles",
    )(rel_bias.astype(F32), idx)


def _attn_kernel(lamv_ref, gain_ref, bias_ref, q_ref, k_ref, v_ref, o_ref,
                 vt_ref, qt_ref, sa_ref, sb_ref, mxa_ref, mxb_ref, m_ref, l_ref, acc_ref, *, t, lam_init):
    qi = pl.program_id(2)
    d = DIFF_HEAD_DIM
    hd = 2 * d
    nh, nkv = vt_ref.shape[0], vt_ref.shape[1]
    heads = range(nh)

    def lanes(e):
        return slice(e * hd, (e + 1) * hd)

    @pl.when(qi == 0)
    def _():
        for e in heads:
            for c in range(nkv):
                vt_ref[e, c] = v_ref[c * t:(c + 1) * t, lanes(e)].astype(F32).T.astype(vt_ref.dtype)

    row = lax.broadcasted_iota(jnp.int32, (hd, t), 0)
    for e in heads:
        qT = (q_ref[:, lanes(e)].astype(F32) * (d ** -0.5 * LOG2E)).T
        qt_ref[e, :, 0:t] = jnp.where(row < d, qT, 0.0).astype(qt_ref.dtype)
        qt_ref[e, :, t:2 * t] = jnp.where(row >= d, qT, 0.0).astype(qt_ref.dtype)

    m_ref[...] = jnp.full(m_ref.shape, NEG_BIG, F32)
    l_ref[...] = jnp.zeros(l_ref.shape, F32)
    acc_ref[...] = jnp.zeros(acc_ref.shape, F32)

    buf_a = (sa_ref, mxa_ref)
    buf_b = (sb_ref, mxb_ref)

    def scores(e, j, buf):
        s_ref, mx_ref = buf
        kblk = k_ref[pl.ds(pl.multiple_of(j * t, t), t), lanes(e)]
        s = jnp.dot(kblk, qt_ref[e], preferred_element_type=F32)
        b = bias_ref[e, jnp.clip(j - qi + 2, 0, 2)]
        for c in range(2):
            sc = s[:, c * t:(c + 1) * t] + b
            s_ref[e, :, c * t:(c + 1) * t] = sc
            mx_ref[e, :, c * t:(c + 1) * t] = jnp.max(sc, axis=0, keepdims=True)

    def reduce(e, buf):
        s_ref, mx_ref = buf
        m_prev = m_ref[e]
        m_new = jnp.maximum(m_prev, mx_ref[e])
        alpha = jnp.exp2(m_prev - m_new)
        p = jnp.exp2(s_ref[e] - m_new)
        l_ref[e] = alpha * l_ref[e] + jnp.sum(p, axis=0, keepdims=True)
        m_ref[e] = m_new
        return alpha, p.astype(vt_ref.dtype)

    def accumulate(e, j, alpha, p):
        acc_ref[e] = alpha * acc_ref[e] + jnp.dot(vt_ref[e, j], p, preferred_element_type=F32)

    def step(j, cur, nxt):
        ap = [reduce(e, cur) for e in heads]
        for e in heads:
            scores(e, j + 1, nxt)
        for e in heads:
            accumulate(e, j, *ap[e])

    odd = (qi & 1) == 1

    @pl.when(jnp.logical_not(odd))
    def _():
        for e in heads:
            scores(e, 0, buf_a)

    @pl.when(odd)
    def _():
        for e in heads:
            scores(e, 0, buf_b)
        step(0, buf_b, buf_a)

    def pair(pp, carry):
        j = (qi & 1) + 2 * pp
        step(j, buf_a, buf_b)
        step(j + 1, buf_b, buf_a)
        return carry

    lax.fori_loop(0, lax.shift_right_logical(qi, 1), pair, 0)
    ap = [reduce(e, buf_a) for e in heads]
    for e in heads:
        accumulate(e, qi, *ap[e])

    lv = lamv_ref[...]
    lam = (jnp.exp(jnp.sum(lv[0:1] * lv[1:2], axis=-1, keepdims=True))
           - jnp.exp(jnp.sum(lv[2:3] * lv[3:4], axis=-1, keepdims=True)) + lam_init)
    for e in heads:
        on = acc_ref[e] * (1.0 / l_ref[e])
        o = on[:, 0:t] - lam * on[:, t:2 * t]
        o = o * lax.rsqrt(jnp.mean(o * o, axis=0, keepdims=True) + EPS)
        o = (o * gain_ref[...]) * (1.0 - lam_init)
        o_ref[:, lanes(e)] = o.T.astype(o_ref.dtype)


def diff_attention(qkv, bias, rel_bias, lamv, gain, *, batch, seq, lam_init, t):
    n_heads = rel_bias.shape[1]
    hd = 2 * DIFF_HEAD_DIM
    nq = seq // t
    nh = 2 if n_heads % 2 == 0 else 1
    ng = n_heads // nh
    kernel = functools.partial(_attn_kernel, t=t, lam_init=lam_init)
    stat = pltpu.VMEM((nh, 1, 2 * t), F32)
    tile = pltpu.VMEM((nh, t, 2 * t), F32)
    return pl.pallas_call(
        kernel,
        grid=(batch, ng, nq),
        in_specs=[pl.BlockSpec((4, DIFF_HEAD_DIM), lambda b, h, i: (0, 0)),
                  pl.BlockSpec((hd, 1), lambda b, h, i: (0, 0)),
                  pl.BlockSpec((nh, 3, t, t), lambda b, h, i: (h, 0, 0, 0)),
                  pl.BlockSpec((t, nh * hd), lambda b, h, i: (b * nq + i, h)),
                  pl.BlockSpec((seq, nh * hd), lambda b, h, i: (b, ng + h)),
                  pl.BlockSpec((seq, nh * hd), lambda b, h, i: (b, 2 * ng + h))],
        out_specs=pl.BlockSpec((t, nh * hd), lambda b, h, i: (b * nq + i, h)),
        out_shape=jax.ShapeDtypeStruct((batch * seq, n_heads * hd), BF16),
        scratch_shapes=[pltpu.VMEM((nh, nq, hd, t), BF16), pltpu.VMEM((nh, hd, 2 * t), BF16),
                        tile, tile, stat, stat, stat, stat,
                        pltpu.VMEM((nh, hd, 2 * t), F32)],
        compiler_params=_cparams("parallel", "parallel", "arbitrary"),
        name="diff_attention",
    )(lamv, gain.reshape(hd, 1).astype(F32), bias, qkv, qkv, qkv)


def _s5_prep_kernel(lre_ref, lim_ref, ldt_ref, btr_ref, bti_ref, cr_ref, ci_ref,
                    tg_ref, wg_ref, vgt_ref, ap_ref, e_ref, q_ref, *, gb, chunk, nlev):
    L = chunk
    P = lre_ref.shape[-1]
    W = SSM_GROUP * L
    rows = e_ref.shape[1]
    lane = lax.broadcasted_iota(jnp.int32, (1, 2 * P), 1)
    sgn = jnp.where(lane < P, -1.0, 1.0).astype(F32)
    jj = lax.broadcasted_iota(jnp.int32, (rows, 2 * P), 0).astype(F32)

    def dup(x):
        return jnp.concatenate([x, x], axis=-1)

    def group(g, carry):
        dt = jnp.exp(ldt_ref[g])
        lr, li = lre_ref[g], lim_ref[g]
        lrdt, lidt = lr * dt, li * dt
        mag = jnp.exp(lrdt)
        ar, ai = mag * jnp.cos(lidt), mag * jnp.sin(lidt)
        den = lr * lr + li * li
        nr, ni = ar - 1.0, ai
        gr = (nr * lr + ni * li) / den
        gi = (ni * lr - nr * li) / den
        btr, bti = btr_ref[g], bti_ref[g]
        bbr = gr * btr - gi * bti
        bbi = gr * bti + gi * btr
        cr, ci = cr_ref[g], ci_ref[g]
        bc1 = jnp.concatenate([bbr, bbi], axis=-1)
        bc2 = jnp.concatenate([-bbi, bbr], axis=-1)
        cc1 = jnp.concatenate([cr, -ci], axis=-1)
        cc2 = jnp.concatenate([-ci, -cr], axis=-1)

        mg = jnp.exp(jj * dup(lrdt))
        ang = jj * dup(lidt)
        e_ref[0] = mg * jnp.cos(ang)
        e_ref[1] = mg * jnp.sin(ang)

        for j in range(L + 1):
            er = jnp.broadcast_to(e_ref[0, j:j + 1, :], (SSM_GROUP, 2 * P))
            ei = jnp.broadcast_to(e_ref[1, j:j + 1, :], (SSM_GROUP, 2 * P))
            q_ref[j * SSM_GROUP:(j + 1) * SSM_GROUP, :] = er * cc1 + ei * cc2
            if j < L:
                tau = L - 1 - j
                wg_ref[g, tau * SSM_GROUP:(tau + 1) * SSM_GROUP, :] = (er * bc1 + ei * bc2).astype(wg_ref.dtype)

        vgt_ref[g] = q_ref[SSM_GROUP:SSM_GROUP * (L + 1), :].astype(vgt_ref.dtype)
        strip = lax.dot_general(bc1, q_ref[0:W, :], (((1,), (1,)), ((), ())),
                                precision=lax.Precision.HIGHEST, preferred_element_type=F32)
        pad = jnp.concatenate([jnp.zeros_like(strip), strip], axis=-1)
        lanes_per_step = LANES // SSM_GROUP
        for r in range(lanes_per_step):
            rolled = pad if r == 0 else pltpu.roll(pad, SSM_GROUP * r, axis=1)
            for qd in range(L // lanes_per_step):
                tau = lanes_per_step * qd + r
                tg_ref[g, tau * SSM_GROUP:(tau + 1) * SSM_GROUP, :] = (
                    rolled[:, W - LANES * qd:2 * W - LANES * qd].astype(tg_ref.dtype))

        pr, pi = e_ref[0, L:L + 1, :], e_ref[1, L:L + 1, :]
        for k in range(nlev):
            ap_ref[g, k, 0:1, :] = pr
            ap_ref[g, k, 1:2, :] = pi * sgn
            pr, pi = pr * pr - pi * pi, 2.0 * pr * pi
        return carry

    lax.fori_loop(0, gb, group, 0)


def s5_prep(lam_re, lam_im, log_dt, bt_re, bt_im, c_re, c_im, *, chunk, nlev, gb=8):
    G, P = lam_re.shape
    H = SSM_GROUP
    W = H * chunk
    assert 2 * P == LANES and chunk % (LANES // H) == 0
    gb = _tile(G, gb)
    rows = -(-(chunk + 1) // 8) * 8
    vec = lambda a: a.reshape(G, 1, -1).astype(F32)
    g3 = lambda s1, s2: pl.BlockSpec((gb, s1, s2), lambda i: (i, 0, 0))
    return pl.pallas_call(
        functools.partial(_s5_prep_kernel, gb=gb, chunk=chunk, nlev=nlev),
        grid=(G // gb,),
        in_specs=[g3(1, P), g3(1, P), g3(1, 1), g3(H, P), g3(H, P), g3(H, P), g3(H, P)],
        out_specs=[g3(W, W), g3(W, 2 * P), g3(W, 2 * P),
                   pl.BlockSpec((gb, nlev, 2, 2 * P), lambda i: (i, 0, 0, 0))],
        out_shape=[jax.ShapeDtypeStruct((G, W, W), BF16),
                   jax.ShapeDtypeStruct((G, W, 2 * P), BF16),
                   jax.ShapeDtypeStruct((G, W, 2 * P), BF16),
                   jax.ShapeDtypeStruct((G, nlev, 2, 2 * P), F32)],
        scratch_shapes=[pltpu.VMEM((2, rows, 2 * P), F32),
                        pltpu.VMEM((H * (chunk + 1), 2 * P), F32)],
        compiler_params=_cparams("parallel"),
        name="s5_prep",
    )(vec(lam_re), vec(lam_im), vec(log_dt), bt_re.astype(F32), bt_im.astype(F32),
      c_re.astype(F32), c_im.astype(F32))


def _s5_kernel(x_ref, perm_ref, tg_ref, wg_ref, vgt_ref, ap_ref, d_ref, o_ref, xs_ref, ys_ref,
               *, gb, nc, nlev, na):
    n = x_ref.shape[1]
    P2 = wg_ref.shape[-1]
    cidx = lax.broadcasted_iota(jnp.int32, (n, P2), 0) % nc

    def shift_rows(s, sh):
        return jnp.where(cidx >= sh, pltpu.roll(s, sh, axis=0), 0.0)

    for a in range(na):
        seg = jnp.concatenate([x_ref[OCTET * a + b] for b in range(OCTET)], axis=-1)
        seg = jnp.dot(seg, perm_ref[...], preferred_element_type=F32).astype(xs_ref.dtype)
        for g in range(gb):
            xs_ref[g, a] = seg[:, g * LANES:(g + 1) * LANES]

    def group(g, carry):
        xb = jnp.concatenate([xs_ref[g, a] for a in range(na)], axis=-1)
        s = jnp.dot(xb, wg_ref[g], preferred_element_type=F32)
        for k in range(nlev):
            sh = shift_rows(s, 1 << k)
            s = s + sh * ap_ref[g, k, 0:1, :] + pltpu.roll(sh, P2 // 2, axis=1) * ap_ref[g, k, 1:2, :]
        s_in = shift_rows(s, 1)
        y = jnp.dot(xb, tg_ref[g], preferred_element_type=F32)
        y = y + lax.dot_general(s_in.astype(BF16), vgt_ref[g], (((1,), (1,)), ((), ())),
                                preferred_element_type=F32)
        y = y + xb.astype(F32) * d_ref[g]
        yb = jax.nn.gelu(y).astype(ys_ref.dtype)
        for a in range(na):
            ys_ref[a, g] = yb[:, a * LANES:(a + 1) * LANES]
        return carry

    lax.fori_loop(0, gb, group, 0)

    for a in range(na):
        seg = jnp.concatenate([ys_ref[a, g] for g in range(gb)], axis=-1)
        seg = lax.dot_general(seg, perm_ref[...], (((1,), (1,)), ((), ())), preferred_element_type=F32)
        for b in range(OCTET):
            o_ref[OCTET * a + b] = seg[:, b * LANES:(b + 1) * LANES].astype(o_ref.dtype)


def s5_apply(ut, tg, wg, vgt, ap, dt, *, nc, nlev):
    L, n, width = ut.shape
    G = width // SSM_GROUP
    W = SSM_GROUP * L
    gb = LANES // SSM_GROUP
    assert G % gb == 0 and L % OCTET == 0 and gb * SSM_GROUP * OCTET == OCTET * LANES
    lane = np.arange(OCTET * LANES)
    b, g, h = lane // LANES, (lane % LANES) // SSM_GROUP, lane % SSM_GROUP
    perm = np.zeros((OCTET * LANES, OCTET * LANES), np.float32)
    perm[lane, g * (OCTET * SSM_GROUP) + b * SSM_GROUP + h] = 1.0
    g3 = lambda s1, s2: pl.BlockSpec((gb, s1, s2), lambda i: (i, 0, 0))
    io = pl.BlockSpec((L, n, LANES), lambda i: (0, 0, i))
    return pl.pallas_call(
        functools.partial(_s5_kernel, gb=gb, nc=nc, nlev=nlev, na=L // OCTET),
        grid=(G // gb,),
        in_specs=[io, pl.BlockSpec(perm.shape, lambda i: (0, 0)), g3(W, W), g3(W, LANES), g3(W, LANES),
                  pl.BlockSpec((gb, nlev, 2, LANES), lambda i: (i, 0, 0, 0)), g3(1, W)],
        out_specs=io,
        out_shape=jax.ShapeDtypeStruct((L, n, width), BF16),
        scratch_shapes=[pltpu.VMEM((gb, L // OCTET, n, LANES), BF16),
                        pltpu.VMEM((L // OCTET, gb, n, LANES), BF16)],
        compiler_params=_cparams("parallel"),
        name="s5_apply",
    )(ut, jnp.asarray(perm, BF16), tg, wg, vgt, ap, dt)


def s5_layer(u, lam_re, lam_im, log_dt, b_re, b_im, c_re, c_im, d, *, batch, seq):
    T, width = u.shape
    G = width // SSM_GROUP
    L = _tile(seq, SSM_CHUNK)
    nc = seq // L
    nlev = max(nc - 1, 0).bit_length()
    n = batch * nc
    W = SSM_GROUP * L
    bt_re = jnp.swapaxes(b_re, -1, -2)
    bt_im = jnp.swapaxes(b_im, -1, -2)
    tg, wg, vgt, ap = s5_prep(lam_re, lam_im, log_dt, bt_re, bt_im, c_re, c_im, chunk=L, nlev=max(nlev, 1))
    ut = u.reshape(n, L, width).swapaxes(0, 1)
    dtile = jnp.tile(d.astype(F32), (1, L)).reshape(G, 1, W)
    yt = s5_apply(ut, tg, wg, vgt, ap, dtile, nc=nc, nlev=nlev)
    return yt.swapaxes(0, 1).reshape(T, width)


@jax.jit
def _forward(x, w_in, w_out, norm_mix, norm_mlp, w_up, w_down, norm_final, rel_bias,
             lambda_q1, lambda_k1, lambda_q2, lambda_k2, subln_gain,
             ssm_lambda_re, ssm_lambda_im, ssm_log_dt, ssm_b_re, ssm_b_im,
             ssm_c_re, ssm_c_im, ssm_d, w_glu, b_glu):
    batch, seq, d_model = x.shape
    depth = w_in.shape[0]
    ssm_width = w_glu.shape[-1]
    attn_width = (w_in.shape[-1] - ssm_width) // 3
    T = batch * seq

    t_attn = _tile(seq, 512)
    bias = bias_tiles(rel_bias, t_attn, seq)

    w_out_b, w_glu_b, w_in_b = w_out.astype(BF16), w_glu.astype(BF16), w_in[0].astype(BF16)
    h = x.reshape(T, d_model).astype(F32)
    for i in range(depth):
        lam_init = 0.8 - 0.6 * math.exp(-0.3 * i)
        xn = rmsnorm(h, norm_mix[i], BF16)
        proj, w_up_b = matmul(xn, w_in_b, out_dtype=BF16, cast=(w_up, i), name="proj_in")
        lamv = jnp.stack([lambda_q1[i], lambda_k1[i], lambda_q2[i], lambda_k2[i]]).astype(F32)
        attn = diff_attention(proj, bias, rel_bias, lamv, subln_gain[i],
                              batch=batch, seq=seq, lam_init=lam_init, t=t_attn)
        y = s5_layer(proj[:, 3 * attn_width:], ssm_lambda_re[i], ssm_lambda_im[i], ssm_log_dt[i],
                     ssm_b_re[i], ssm_b_im[i], ssm_c_re[i], ssm_c_im[i], ssm_d[i], batch=batch, seq=seq)
        if i + 1 < depth:
            ssm, w_in_b = glu(y, w_glu_b, i, b_glu[i], cast=(w_in, i + 1))
        else:
            ssm = glu(y, w_glu_b, i, b_glu[i])
        h = outproj(attn, ssm, w_out_b, i, h)
        xn = rmsnorm(h, norm_mlp[i], BF16)
        ff, w_down_b = matmul(xn, w_up_b, out_dtype=BF16, epilogue="relu2", cast=(w_down, i), name="mlp_up")
        h = matmul(ff, w_down_b, out_dtype=F32, epilogue="residual", res=h, tk=4096, name="mlp_down")
    out = rmsnorm(h, norm_final, x.dtype)
    return out.reshape(batch, seq, d_model)


def kernel(x, w_in, w_out, norm_mix, norm_mlp, w_up, w_down, norm_final, rel_bias, lambda_q1, lambda_k1, lambda_q2, lambda_k2, subln_gain, ssm_lambda_re, ssm_lambda_im, ssm_log_dt, ssm_b_re, ssm_b_im, ssm_c_re, ssm_c_im, ssm_d, w_glu, b_glu):
    return _forward(x, w_in, w_out, norm_mix, norm_mlp, w_up, w_down, norm_final, rel_bias,
                    lambda_q1, lambda_k1, lambda_q2, lambda_k2, subln_gain,
                    ssm_lambda_re, ssm_lambda_im, ssm_log_dt, ssm_b_re, ssm_b_im,
                    ssm_c_re, ssm_c_im, ssm_d, w_glu, b_glu)
```

```python
import functools
import math

import numpy as np
import jax
import jax.numpy as jnp
from jax import lax
from jax.experimental import pallas as pl
from jax.experimental.pallas import tpu as pltpu

F32 = jnp.float32
BF16 = jnp.bfloat16

EPS = 1e-6
MAX_EXACT = 16
MAX_DISTANCE = 128
DIFF_HEAD_DIM = 64
SSM_GROUP = 16
SSM_CHUNK = 32
OCTET = 8
NEG_BIG = -1e30
LOG2E = math.log2(math.e)

V7X_VMEM_LIMIT_BYTES = 56 * 1024 * 1024
LANES = 128
BF16_SUBLANES = 16


def _cparams(*sem):
    return pltpu.CompilerParams(dimension_semantics=sem, vmem_limit_bytes=V7X_VMEM_LIMIT_BYTES)


def _tile(dim, pref):
    t = min(dim, pref)
    while dim % t:
        t //= 2
    return t


def _rmsnorm_kernel(x_ref, g_ref, o_ref):
    x = x_ref[...].astype(F32)
    ms = jnp.mean(x * x, axis=-1, keepdims=True)
    o_ref[...] = ((x * lax.rsqrt(ms + EPS)) * g_ref[...]).astype(o_ref.dtype)


def rmsnorm(x, g, out_dtype):
    m, d = x.shape
    tm = _tile(m, 256)
    return pl.pallas_call(
        _rmsnorm_kernel,
        grid=(m // tm,),
        in_specs=[pl.BlockSpec((tm, d), lambda i: (i, 0)),
                  pl.BlockSpec((1, d), lambda i: (0, 0))],
        out_specs=pl.BlockSpec((tm, d), lambda i: (i, 0)),
        out_shape=jax.ShapeDtypeStruct((m, d), out_dtype),
        compiler_params=_cparams("parallel"),
        name="rmsnorm",
    )(x, g.reshape(1, d).astype(F32))


def _mm_kernel(a_ref, b_ref, *rest, nk, epilogue, cast):
    rest = list(rest)
    r_ref = rest.pop(0) if epilogue == "residual" else None
    if cast:
        c_ref, o_ref, co_ref = rest
        co_ref[...] = c_ref[...].astype(co_ref.dtype)
    else:
        (o_ref,) = rest

    if nk == 1:
        acc = jnp.dot(a_ref[...], b_ref[...], preferred_element_type=F32)
        if epilogue == "relu2":
            acc = jnp.square(jnp.maximum(acc, 0.0))
        elif epilogue == "residual":
            acc = r_ref[...] + acc
        o_ref[...] = acc.astype(o_ref.dtype)
        return

    @pl.when(pl.program_id(2) == 0)
    def _():
        o_ref[...] = r_ref[...]

    o_ref[...] += jnp.dot(a_ref[...], b_ref[...], preferred_element_type=F32)


def matmul(a, w, *, out_dtype, epilogue="none", res=None, cast=None, tm=1024, tn=1024, tk=4096, name="matmul"):
    m, kdim = a.shape
    _, n = w.shape
    tm, tn, tk = _tile(m, tm), _tile(n, tn), _tile(kdim, tk)
    gm, gn, nk = m // tm, n // tn, kdim // tk
    in_specs = [pl.BlockSpec((tm, tk), lambda i, j, k: (i, k)),
                pl.BlockSpec((tk, tn), lambda i, j, k: (k, j))]
    args = [a, w]
    if epilogue == "residual":
        in_specs.append(pl.BlockSpec((tm, tn), lambda i, j, k: (i, j)))
        args.append(res)
    out_specs = [pl.BlockSpec((tm, tn), lambda i, j, k: (i, j))]
    out_shape = [jax.ShapeDtypeStruct((m, n), out_dtype)]
    if cast is not None:
        w_src, layer = cast
        _, ck, cn = w_src.shape
        rows = ck // (gm * gn * nk)
        assert rows * gm * gn * nk == ck and rows % BF16_SUBLANES == 0
        step = lambda i, j, k: (i * gn + j) * nk + k
        in_specs.append(pl.BlockSpec((None, rows, cn), lambda i, j, k: (layer, step(i, j, k), 0)))
        args.append(w_src)
        out_specs.append(pl.BlockSpec((rows, cn), lambda i, j, k: (step(i, j, k), 0)))
        out_shape.append(jax.ShapeDtypeStruct((ck, cn), a.dtype))
    assert nk == 1 or (epilogue == "residual" and out_dtype == F32)
    outs = pl.pallas_call(
        functools.partial(_mm_kernel, nk=nk, epilogue=epilogue, cast=cast is not None),
        grid=(gm, gn, nk),
        in_specs=in_specs,
        out_specs=out_specs,
        out_shape=out_shape,
        compiler_params=_cparams("parallel", "parallel", "arbitrary"),
        name=name,
    )(*args)
    return outs if cast is not None else outs[0]


def _outproj_kernel(a1_ref, a2_ref, b1_ref, b2_ref, r_ref, o_ref):
    acc = jnp.dot(a1_ref[...], b1_ref[...], preferred_element_type=F32)
    acc = acc + jnp.dot(a2_ref[...], b2_ref[...], preferred_element_type=F32)
    o_ref[...] = r_ref[...] + acc


def outproj(a1, a2, w, layer, res, *, tm=1024, tn=1024):
    m, k1 = a1.shape
    _, k2 = a2.shape
    _, _, n = w.shape
    assert k1 == k2
    tm, tn = _tile(m, tm), _tile(n, tn)
    return pl.pallas_call(
        _outproj_kernel,
        grid=(m // tm, n // tn),
        in_specs=[pl.BlockSpec((tm, k1), lambda i, j: (i, 0)),
                  pl.BlockSpec((tm, k2), lambda i, j: (i, 0)),
                  pl.BlockSpec((None, k1, tn), lambda i, j: (layer, 0, j)),
                  pl.BlockSpec((None, k2, tn), lambda i, j: (layer, 1, j)),
                  pl.BlockSpec((tm, tn), lambda i, j: (i, j))],
        out_specs=pl.BlockSpec((tm, tn), lambda i, j: (i, j)),
        out_shape=jax.ShapeDtypeStruct((m, n), F32),
        compiler_params=_cparams("parallel", "parallel"),
        name="outproj",
    )(a1, a2, w, w, res)


def _glu_kernel(y_ref, w_ref, b_ref, *rest, cast):
    if cast:
        c_ref, o_ref, co_ref = rest
        co_ref[...] = c_ref[...].astype(co_ref.dtype)
    else:
        (o_ref,) = rest
    y = y_ref[...]
    z = jnp.dot(y, w_ref[...], preferred_element_type=F32) + b_ref[...]
    o_ref[...] = (y.astype(F32) * jax.nn.sigmoid(z)).astype(o_ref.dtype)


def glu(y, w, layer, b, *, cast=None, tm=512):
    m, n = y.shape
    tm = _tile(m, tm)
    steps = m // tm
    in_specs = [pl.BlockSpec((tm, n), lambda i: (i, 0)),
                pl.BlockSpec((None, n, n), lambda i: (layer, 0, 0)),
                pl.BlockSpec((1, n), lambda i: (0, 0))]
    args = [y, w, b.reshape(1, n).astype(F32)]
    out_specs = [pl.BlockSpec((tm, n), lambda i: (i, 0))]
    out_shape = [jax.ShapeDtypeStruct((m, n), BF16)]
    if cast is not None:
        w_src, src_layer = cast
        _, ck, cn = w_src.shape
        rows = ck // steps
        assert rows * steps == ck and rows % BF16_SUBLANES == 0
        in_specs.append(pl.BlockSpec((None, rows, cn), lambda i: (src_layer, i, 0)))
        args.append(w_src)
        out_specs.append(pl.BlockSpec((rows, cn), lambda i: (i, 0)))
        out_shape.append(jax.ShapeDtypeStruct((ck, cn), y.dtype))
    outs = pl.pallas_call(
        functools.partial(_glu_kernel, cast=cast is not None),
        grid=(steps,),
        in_specs=in_specs,
        out_specs=out_specs,
        out_shape=out_shape,
        compiler_params=_cparams("parallel"),
        name="glu",
    )(*args)
    return outs if cast is not None else outs[0]


def _t5_bucket_np(n, num_buckets):
    n = np.maximum(n, 0)
    nf = np.maximum(n, 1).astype(np.float32)
    large = MAX_EXACT + (np.log(nf / np.float32(MAX_EXACT)) / np.float32(math.log(MAX_DISTANCE / MAX_EXACT))
                         * np.float32(num_buckets - MAX_EXACT)).astype(np.int32)
    large = np.minimum(large, num_buckets - 1)
    return np.where(n < MAX_EXACT, n, large).astype(np.int32)


def _bias_tiles_kernel(rb_ref, idx_ref, o_ref, *, num_buckets):
    h = pl.program_id(0)
    for d in range(idx_ref.shape[0]):
        idx = idx_ref[d]
        acc = jnp.full(idx.shape, NEG_BIG, F32)
        for k in range(num_buckets):
            acc = jnp.where(idx == k, rb_ref[k, h] * LOG2E, acc)
        o_ref[0, d] = acc


def bias_tiles(rel_bias, t, seq):
    num_buckets, n_heads = rel_bias.shape
    c = np.arange(t)[:, None]
    r = np.arange(t)[None, :]
    far = _t5_bucket_np(np.arange(t + 1, max(seq, t + 2)), num_buckets)
    assert np.all(far == far[0]), "kv tiles two or more steps left of the diagonal must share one bucket"
    idx_far = np.full((t, t), far[0])
    idx_left = _t5_bucket_np(t + r - c, num_buckets)
    idx_diag = np.where(r - c >= 0, _t5_bucket_np(r - c, num_buckets), -1)
    idx = jnp.asarray(np.stack([idx_far, idx_left, idx_diag]).astype(np.int32))
    return pl.pallas_call(
        functools.partial(_bias_tiles_kernel, num_buckets=num_buckets),
        grid=(n_heads,),
        in_specs=[pl.BlockSpec(memory_space=pltpu.SMEM),
                  pl.BlockSpec((3, t, t), lambda h: (0, 0, 0))],
        out_specs=pl.BlockSpec((1, 3, t, t), lambda h: (h, 0, 0, 0)),
        out_shape=jax.ShapeDtypeStruct((n_heads, 3, t, t), F32),
        compiler_params=_cparams("arbitrary"),
        name="bias_tiles",
    )(rel_bias.astype(F32), idx)


def _attn_kernel(lamv_ref, gain_ref, bias_ref, q_ref, k_ref, v_ref, o_ref,
                 vt_ref, qt_ref, sa_ref, sb_ref, mxa_ref, mxb_ref, m_ref, acc_ref, *, t, lam_init):
    qi = pl.program_id(2)
    d = DIFF_HEAD_DIM
    hd = 2 * d
    nh, nkv = vt_ref.shape[0], vt_ref.shape[1]
    heads = range(nh)

    def lanes(e):
        return slice(e * hd, (e + 1) * hd)

    ones_row = lax.broadcasted_iota(jnp.int32, (BF16_SUBLANES, t), 0) == 0

    @pl.when(qi == 0)
    def _():
        for e in heads:
            for c in range(nkv):
                vt_ref[e, c, 0:hd] = v_ref[c * t:(c + 1) * t, lanes(e)].astype(F32).T.astype(vt_ref.dtype)
                vt_ref[e, c, hd:] = jnp.where(ones_row, 1.0, 0.0).astype(vt_ref.dtype)

    row = lax.broadcasted_iota(jnp.int32, (hd, t), 0)
    for e in heads:
        qT = (q_ref[:, lanes(e)].astype(F32) * (d ** -0.5 * LOG2E)).T
        qt_ref[e, :, 0:t] = jnp.where(row < d, qT, 0.0).astype(qt_ref.dtype)
        qt_ref[e, :, t:2 * t] = jnp.where(row >= d, qT, 0.0).astype(qt_ref.dtype)

    m_ref[...] = jnp.full(m_ref.shape, NEG_BIG, F32)
    acc_ref[...] = jnp.zeros(acc_ref.shape, F32)

    buf_a = (sa_ref, mxa_ref)
    buf_b = (sb_ref, mxb_ref)

    def scores(e, j, buf):
        s_ref, mx_ref = buf
        kblk = k_ref[pl.ds(pl.multiple_of(j * t, t), t), lanes(e)]
        s = jnp.dot(kblk, qt_ref[e], preferred_element_type=F32)
        b = bias_ref[e, jnp.clip(j - qi + 2, 0, 2)]
        for c in range(2):
            sc = s[:, c * t:(c + 1) * t] + b
            s_ref[e, :, c * t:(c + 1) * t] = sc
            mx_ref[e, :, c * t:(c + 1) * t] = jnp.max(sc, axis=0, keepdims=True)

    def reduce(e, buf):
        s_ref, mx_ref = buf
        m_prev = m_ref[e]
        m_new = jnp.maximum(m_prev, mx_ref[e])
        alpha = jnp.exp2(m_prev - m_new)
        p = jnp.exp2((s_ref[e] - m_new).astype(vt_ref.dtype))
        m_ref[e] = m_new
        return alpha, p

    def accumulate(e, j, alpha, p):
        acc_ref[e] = alpha * acc_ref[e] + jnp.dot(vt_ref[e, j], p, preferred_element_type=F32)

    def step(j, cur, nxt):
        ap = [reduce(e, cur) for e in heads]
        for e in heads:
            scores(e, j + 1, nxt)
        for e in heads:
            accumulate(e, j, *ap[e])

    odd = (qi & 1) == 1

    @pl.when(jnp.logical_not(odd))
    def _():
        for e in heads:
            scores(e, 0, buf_a)

    @pl.when(odd)
    def _():
        for e in heads:
            scores(e, 0, buf_b)
        step(0, buf_b, buf_a)

    def pair(pp, carry):
        j = (qi & 1) + 2 * pp
        step(j, buf_a, buf_b)
        step(j + 1, buf_b, buf_a)
        return carry

    lax.fori_loop(0, lax.shift_right_logical(qi, 1), pair, 0)
    ap = [reduce(e, buf_a) for e in heads]
    for e in heads:
        accumulate(e, qi, *ap[e])

    lv = lamv_ref[...]
    lam = (jnp.exp(jnp.sum(lv[0:1] * lv[1:2], axis=-1, keepdims=True))
           - jnp.exp(jnp.sum(lv[2:3] * lv[3:4], axis=-1, keepdims=True)) + lam_init)
    for e in heads:
        on = acc_ref[e, 0:hd] * (1.0 / acc_ref[e, hd:hd + 1])
        o = on[:, 0:t] - lam * on[:, t:2 * t]
        o = o * lax.rsqrt(jnp.mean(o * o, axis=0, keepdims=True) + EPS)
        o = (o * gain_ref[...]) * (1.0 - lam_init)
        o_ref[:, lanes(e)] = o.T.astype(o_ref.dtype)


def diff_attention(qkv, bias, rel_bias, lamv, gain, *, batch, seq, lam_init, t):
    n_heads = rel_bias.shape[1]
    hd = 2 * DIFF_HEAD_DIM
    nq = seq // t
    nh = 2 if n_heads % 2 == 0 else 1
    ng = n_heads // nh
    kernel = functools.partial(_attn_kernel, t=t, lam_init=lam_init)
    stat = pltpu.VMEM((nh, 1, 2 * t), F32)
    tile = pltpu.VMEM((nh, t, 2 * t), F32)
    return pl.pallas_call(
        kernel,
        grid=(batch, ng, nq),
        in_specs=[pl.BlockSpec((4, DIFF_HEAD_DIM), lambda b, h, i: (0, 0)),
                  pl.BlockSpec((hd, 1), lambda b, h, i: (0, 0)),
                  pl.BlockSpec((nh, 3, t, t), lambda b, h, i: (h, 0, 0, 0)),
                  pl.BlockSpec((t, nh * hd), lambda b, h, i: (b * nq + i, h)),
                  pl.BlockSpec((seq, nh * hd), lambda b, h, i: (b, ng + h)),
                  pl.BlockSpec((seq, nh * hd), lambda b, h, i: (b, 2 * ng + h))],
        out_specs=pl.BlockSpec((t, nh * hd), lambda b, h, i: (b * nq + i, h)),
        out_shape=jax.ShapeDtypeStruct((batch * seq, n_heads * hd), BF16),
        scratch_shapes=[pltpu.VMEM((nh, nq, hd + BF16_SUBLANES, t), BF16), pltpu.VMEM((nh, hd, 2 * t), BF16),
                        tile, tile, stat, stat, stat,
                        pltpu.VMEM((nh, hd + BF16_SUBLANES, 2 * t), F32)],
        compiler_params=_cparams("parallel", "parallel", "arbitrary"),
        name="diff_attention",
    )(lamv, gain.reshape(hd, 1).astype(F32), bias, qkv, qkv, qkv)


def _s5_prep_kernel(lre_ref, lim_ref, ldt_ref, btr_ref, bti_ref, cr_ref, ci_ref,
                    tg_ref, wg_ref, vgt_ref, ap_ref, e_ref, q_ref, *, gb, chunk, nlev):
    L = chunk
    P = lre_ref.shape[-1]
    W = SSM_GROUP * L
    rows = e_ref.shape[1]
    lane = lax.broadcasted_iota(jnp.int32, (1, 2 * P), 1)
    sgn = jnp.where(lane < P, -1.0, 1.0).astype(F32)
    jj = lax.broadcasted_iota(jnp.int32, (rows, 2 * P), 0).astype(F32)

    def dup(x):
        return jnp.concatenate([x, x], axis=-1)

    def group(g, carry):
        dt = jnp.exp(ldt_ref[g])
        lr, li = lre_ref[g], lim_ref[g]
        lrdt, lidt = lr * dt, li * dt
        mag = jnp.exp(lrdt)
        ar, ai = mag * jnp.cos(lidt), mag * jnp.sin(lidt)
        den = lr * lr + li * li
        nr, ni = ar - 1.0, ai
        gr = (nr * lr + ni * li) / den
        gi = (ni * lr - nr * li) / den
        btr, bti = btr_ref[g], bti_ref[g]
        bbr = gr * btr - gi * bti
        bbi = gr * bti + gi * btr
        cr, ci = cr_ref[g], ci_ref[g]
        bc1 = jnp.concatenate([bbr, bbi], axis=-1)
        bc2 = jnp.concatenate([-bbi, bbr], axis=-1)
        cc1 = jnp.concatenate([cr, -ci], axis=-1)
        cc2 = jnp.concatenate([-ci, -cr], axis=-1)

        mg = jnp.exp(jj * dup(lrdt))
        ang = jj * dup(lidt)
        e_ref[0] = mg * jnp.cos(ang)
        e_ref[1] = mg * jnp.sin(ang)

        for j in range(L + 1):
            er = jnp.broadcast_to(e_ref[0, j:j + 1, :], (SSM_GROUP, 2 * P))
            ei = jnp.broadcast_to(e_ref[1, j:j + 1, :], (SSM_GROUP, 2 * P))
            q_ref[j * SSM_GROUP:(j + 1) * SSM_GROUP, :] = er * cc1 + ei * cc2
            if j < L:
                tau = L - 1 - j
                wg_ref[g, tau * SSM_GROUP:(tau + 1) * SSM_GROUP, :] = (er * bc1 + ei * bc2).astype(wg_ref.dtype)

        vgt_ref[g] = q_ref[SSM_GROUP:SSM_GROUP * (L + 1), :].astype(vgt_ref.dtype)
        strip = lax.dot_general(bc1, q_ref[0:W, :], (((1,), (1,)), ((), ())),
                                precision=lax.Precision.HIGHEST, preferred_element_type=F32)
        pad = jnp.concatenate([jnp.zeros_like(strip), strip], axis=-1)
        lanes_per_step = LANES // SSM_GROUP
        for r in range(lanes_per_step):
            rolled = pad if r == 0 else pltpu.roll(pad, SSM_GROUP * r, axis=1)
            for qd in range(L // lanes_per_step):
                tau = lanes_per_step * qd + r
                tg_ref[g, tau * SSM_GROUP:(tau + 1) * SSM_GROUP, :] = (
                    rolled[:, W - LANES * qd:2 * W - LANES * qd].astype(tg_ref.dtype))

        pr, pi = e_ref[0, L:L + 1, :], e_ref[1, L:L + 1, :]
        for k in range(nlev):
            ap_ref[g, k, 0:1, :] = pr
            ap_ref[g, k, 1:2, :] = pi * sgn
            pr, pi = pr * pr - pi * pi, 2.0 * pr * pi
        return carry

    lax.fori_loop(0, gb, group, 0)


def s5_prep(lam_re, lam_im, log_dt, bt_re, bt_im, c_re, c_im, *, chunk, nlev, gb=8):
    G, P = lam_re.shape
    H = SSM_GROUP
    W = H * chunk
    assert 2 * P == LANES and chunk % (LANES // H) == 0
    gb = _tile(G, gb)
    rows = -(-(chunk + 1) // 8) * 8
    vec = lambda a: a.reshape(G, 1, -1).astype(F32)
    g3 = lambda s1, s2: pl.BlockSpec((gb, s1, s2), lambda i: (i, 0, 0))
    return pl.pallas_call(
        functools.partial(_s5_prep_kernel, gb=gb, chunk=chunk, nlev=nlev),
        grid=(G // gb,),
        in_specs=[g3(1, P), g3(1, P), g3(1, 1), g3(H, P), g3(H, P), g3(H, P), g3(H, P)],
        out_specs=[g3(W, W), g3(W, 2 * P), g3(W, 2 * P),
                   pl.BlockSpec((gb, nlev, 2, 2 * P), lambda i: (i, 0, 0, 0))],
        out_shape=[jax.ShapeDtypeStruct((G, W, W), BF16),
                   jax.ShapeDtypeStruct((G, W, 2 * P), BF16),
                   jax.ShapeDtypeStruct((G, W, 2 * P), BF16),
                   jax.ShapeDtypeStruct((G, nlev, 2, 2 * P), F32)],
        scratch_shapes=[pltpu.VMEM((2, rows, 2 * P), F32),
                        pltpu.VMEM((H * (chunk + 1), 2 * P), F32)],
        compiler_params=_cparams("parallel"),
        name="s5_prep",
    )(vec(lam_re), vec(lam_im), vec(log_dt), bt_re.astype(F32), bt_im.astype(F32),
      c_re.astype(F32), c_im.astype(F32))


def _s5_kernel(x_ref, perm_ref, tg_ref, wg_ref, vgt_ref, ap_ref, d_ref, o_ref, xs_ref, ys_ref,
               *, gb, nc, nlev, na):
    n = x_ref.shape[1]
    P2 = wg_ref.shape[-1]
    cidx = lax.broadcasted_iota(jnp.int32, (n, P2), 0) % nc

    def shift_rows(s, sh):
        return jnp.where(cidx >= sh, pltpu.roll(s, sh, axis=0), 0.0)

    for a in range(na):
        seg = jnp.concatenate([x_ref[OCTET * a + b] for b in range(OCTET)], axis=-1)
        seg = jnp.dot(seg, perm_ref[...], preferred_element_type=F32).astype(xs_ref.dtype)
        for g in range(gb):
            xs_ref[g, a] = seg[:, g * LANES:(g + 1) * LANES]

    def group(g, carry):
        xb = jnp.concatenate([xs_ref[g, a] for a in range(na)], axis=-1)
        s = jnp.dot(xb, wg_ref[g], preferred_element_type=F32)
        for k in range(nlev):
            sh = shift_rows(s, 1 << k)
            s = s + sh * ap_ref[g, k, 0:1, :] + pltpu.roll(sh, P2 // 2, axis=1) * ap_ref[g, k, 1:2, :]
        s_in = shift_rows(s, 1)
        y = jnp.dot(xb, tg_ref[g], preferred_element_type=F32)
        y = y + lax.dot_general(s_in.astype(BF16), vgt_ref[g], (((1,), (1,)), ((), ())),
                                preferred_element_type=F32)
        y = y + xb.astype(F32) * d_ref[g]
        yb = jax.nn.gelu(y).astype(ys_ref.dtype)
        for a in range(na):
            ys_ref[a, g] = yb[:, a * LANES:(a + 1) * LANES]
        return carry

    lax.fori_loop(0, gb, group, 0)

    for a in range(na):
        seg = jnp.concatenate([ys_ref[a, g] for g in range(gb)], axis=-1)
        seg = lax.dot_general(seg, perm_ref[...], (((1,), (1,)), ((), ())), preferred_element_type=F32)
        for b in range(OCTET):
            o_ref[OCTET * a + b] = seg[:, b * LANES:(b + 1) * LANES].astype(o_ref.dtype)


def s5_apply(ut, tg, wg, vgt, ap, dt, *, nc, nlev):
    L, n, width = ut.shape
    G = width // SSM_GROUP
    W = SSM_GROUP * L
    gb = LANES // SSM_GROUP
    assert G % gb == 0 and L % OCTET == 0 and gb * SSM_GROUP * OCTET == OCTET * LANES
    lane = np.arange(OCTET * LANES)
    b, g, h = lane // LANES, (lane % LANES) // SSM_GROUP, lane % SSM_GROUP
    perm = np.zeros((OCTET * LANES, OCTET * LANES), np.float32)
    perm[lane, g * (OCTET * SSM_GROUP) + b * SSM_GROUP + h] = 1.0
    g3 = lambda s1, s2: pl.BlockSpec((gb, s1, s2), lambda i: (i, 0, 0))
    io = pl.BlockSpec((L, n, LANES), lambda i: (0, 0, i))
    return pl.pallas_call(
        functools.partial(_s5_kernel, gb=gb, nc=nc, nlev=nlev, na=L // OCTET),
        grid=(G // gb,),
        in_specs=[io, pl.BlockSpec(perm.shape, lambda i: (0, 0)), g3(W, W), g3(W, LANES), g3(W, LANES),
                  pl.BlockSpec((gb, nlev, 2, LANES), lambda i: (i, 0, 0, 0)), g3(1, W)],
        out_specs=io,
        out_shape=jax.ShapeDtypeStruct((L, n, width), BF16),
        scratch_shapes=[pltpu.VMEM((gb, L // OCTET, n, LANES), BF16),
                        pltpu.VMEM((L // OCTET, gb, n, LANES), BF16)],
        compiler_params=_cparams("parallel"),
        name="s5_apply",
    )(ut, jnp.asarray(perm, BF16), tg, wg, vgt, ap, dt)


def s5_layer(u, lam_re, lam_im, log_dt, b_re, b_im, c_re, c_im, d, *, batch, seq):
    T, width = u.shape
    G = width // SSM_GROUP
    L = _tile(seq, SSM_CHUNK)
    nc = seq // L
    nlev = max(nc - 1, 0).bit_length()
    n = batch * nc
    W = SSM_GROUP * L
    bt_re = jnp.swapaxes(b_re, -1, -2)
    bt_im = jnp.swapaxes(b_im, -1, -2)
    tg, wg, vgt, ap = s5_prep(lam_re, lam_im, log_dt, bt_re, bt_im, c_re, c_im, chunk=L, nlev=max(nlev, 1))
    ut = u.reshape(n, L, width).swapaxes(0, 1)
    dtile = jnp.tile(d.astype(F32), (1, L)).reshape(G, 1, W)
    yt = s5_apply(ut, tg, wg, vgt, ap, dtile, nc=nc, nlev=nlev)
    return yt.swapaxes(0, 1).reshape(T, width)


@jax.jit
def _forward(x, w_in, w_out, norm_mix, norm_mlp, w_up, w_down, norm_final, rel_bias,
             lambda_q1, lambda_k1, lambda_q2, lambda_k2, subln_gain,
             ssm_lambda_re, ssm_lambda_im, ssm_log_dt, ssm_b_re, ssm_b_im,
             ssm_c_re, ssm_c_im, ssm_d, w_glu, b_glu):
    batch, seq, d_model = x.shape
    depth = w_in.shape[0]
    ssm_width = w_glu.shape[-1]
    attn_width = (w_in.shape[-1] - ssm_width) // 3
    T = batch * seq

    t_attn = _tile(seq, 512)
    bias = bias_tiles(rel_bias, t_attn, seq)

    w_out_b, w_glu_b, w_in_b = w_out.astype(BF16), w_glu.astype(BF16), w_in[0].astype(BF16)
    h = x.reshape(T, d_model).astype(F32)
    for i in range(depth):
        lam_init = 0.8 - 0.6 * math.exp(-0.3 * i)
        xn = rmsnorm(h, norm_mix[i], BF16)
        proj, w_up_b = matmul(xn, w_in_b, out_dtype=BF16, cast=(w_up, i), name="proj_in")
        lamv = jnp.stack([lambda_q1[i], lambda_k1[i], lambda_q2[i], lambda_k2[i]]).astype(F32)
        attn = diff_attention(proj, bias, rel_bias, lamv, subln_gain[i],
                              batch=batch, seq=seq, lam_init=lam_init, t=t_attn)
        y = s5_layer(proj[:, 3 * attn_width:], ssm_lambda_re[i], ssm_lambda_im[i], ssm_log_dt[i],
                     ssm_b_re[i], ssm_b_im[i], ssm_c_re[i], ssm_c_im[i], ssm_d[i], batch=batch, seq=seq)
        if i + 1 < depth:
            ssm, w_in_b = glu(y, w_glu_b, i, b_glu[i], cast=(w_in, i + 1))
        else:
            ssm = glu(y, w_glu_b, i, b_glu[i])
        h = outproj(attn, ssm, w_out_b, i, h)
        xn = rmsnorm(h, norm_mlp[i], BF16)
        ff, w_down_b = matmul(xn, w_up_b, out_dtype=BF16, epilogue="relu2", cast=(w_down, i), name="mlp_up")
        h = matmul(ff, w_down_b, out_dtype=F32, epilogue="residual", res=h, tk=4096, name="mlp_down")
    out = rmsnorm(h, norm_final, x.dtype)
    return out.reshape(batch, seq, d_model)


def kernel(x, w_in, w_out, norm_mix, norm_mlp, w_up, w_down, norm_final, rel_bias, lambda_q1, lambda_k1, lambda_q2, lambda_k2, subln_gain, ssm_lambda_re, ssm_lambda_im, ssm_log_dt, ssm_b_re, ssm_b_im, ssm_c_re, ssm_c_im, ssm_d, w_glu, b_glu):
    return _forward(x, w_in, w_out, norm_mix, norm_mlp, w_up, w_down, norm_final, rel_bias,
                    lambda_q1, lambda_k1, lambda_q2, lambda_k2, subln_gain,
                    ssm_lambda_re, ssm_lambda_im, ssm_log_dt, ssm_b_re, ssm_b_im,
                    ssm_c_re, ssm_c_im, ssm_d, w_glu, b_glu)
```

```python
import functools
import math

import numpy as np
import jax
import jax.numpy as jnp
from jax import lax
from jax.experimental import pallas as pl
from jax.experimental.pallas import tpu as pltpu

F32 = jnp.float32
BF16 = jnp.bfloat16

EPS = 1e-6
MAX_EXACT = 16
MAX_DISTANCE = 128
DIFF_HEAD_DIM = 64
SSM_GROUP = 16
SSM_CHUNK = 32
OCTET = 8
GROUPS_PER_ITER = 4
NEG_BIG = -1e30
LOG2E = math.log2(math.e)

V7X_VMEM_LIMIT_BYTES = 56 * 1024 * 1024
LANES = 128
BF16_SUBLANES = 16


def _cparams(*sem):
    return pltpu.CompilerParams(dimension_semantics=sem, vmem_limit_bytes=V7X_VMEM_LIMIT_BYTES)


def _tile(dim, pref):
    t = min(dim, pref)
    while dim % t:
        t //= 2
    return t


def _rmsnorm_kernel(x_ref, g_ref, o_ref):
    x = x_ref[...].astype(F32)
    ms = jnp.mean(x * x, axis=-1, keepdims=True)
    o_ref[...] = ((x * lax.rsqrt(ms + EPS)) * g_ref[...]).astype(o_ref.dtype)


def rmsnorm(x, g, out_dtype):
    m, d = x.shape
    tm = _tile(m, 256)
    return pl.pallas_call(
        _rmsnorm_kernel,
        grid=(m // tm,),
        in_specs=[pl.BlockSpec((tm, d), lambda i: (i, 0)),
                  pl.BlockSpec((1, d), lambda i: (0, 0))],
        out_specs=pl.BlockSpec((tm, d), lambda i: (i, 0)),
        out_shape=jax.ShapeDtypeStruct((m, d), out_dtype),
        compiler_params=_cparams("parallel"),
        name="rmsnorm",
    )(x, g.reshape(1, d).astype(F32))


def _mm_kernel(a_ref, b_ref, *rest, nk, epilogue, cast):
    rest = list(rest)
    r_ref = rest.pop(0) if epilogue == "residual" else None
    if cast:
        c_ref, o_ref, co_ref = rest
        co_ref[...] = c_ref[...].astype(co_ref.dtype)
    else:
        (o_ref,) = rest

    if nk == 1:
        acc = jnp.dot(a_ref[...], b_ref[...], preferred_element_type=F32)
        if epilogue == "relu2":
            acc = jnp.square(jnp.maximum(acc, 0.0))
        elif epilogue == "residual":
            acc = r_ref[...] + acc
        o_ref[...] = acc.astype(o_ref.dtype)
        return

    @pl.when(pl.program_id(2) == 0)
    def _():
        o_ref[...] = r_ref[...]

    o_ref[...] += jnp.dot(a_ref[...], b_ref[...], preferred_element_type=F32)


def matmul(a, w, *, out_dtype, epilogue="none", res=None, cast=None, tm=1024, tn=1024, tk=4096, name="matmul"):
    m, kdim = a.shape
    _, n = w.shape
    tm, tn, tk = _tile(m, tm), _tile(n, tn), _tile(kdim, tk)
    gm, gn, nk = m // tm, n // tn, kdim // tk
    in_specs = [pl.BlockSpec((tm, tk), lambda i, j, k: (i, k)),
                pl.BlockSpec((tk, tn), lambda i, j, k: (k, j))]
    args = [a, w]
    if epilogue == "residual":
        in_specs.append(pl.BlockSpec((tm, tn), lambda i, j, k: (i, j)))
        args.append(res)
    out_specs = [pl.BlockSpec((tm, tn), lambda i, j, k: (i, j))]
    out_shape = [jax.ShapeDtypeStruct((m, n), out_dtype)]
    if cast is not None:
        w_src, layer = cast
        _, ck, cn = w_src.shape
        rows = ck // (gm * gn * nk)
        assert rows * gm * gn * nk == ck and rows % BF16_SUBLANES == 0
        step = lambda i, j, k: (i * gn + j) * nk + k
        in_specs.append(pl.BlockSpec((None, rows, cn), lambda i, j, k: (layer, step(i, j, k), 0)))
        args.append(w_src)
        out_specs.append(pl.BlockSpec((rows, cn), lambda i, j, k: (step(i, j, k), 0)))
        out_shape.append(jax.ShapeDtypeStruct((ck, cn), a.dtype))
    assert nk == 1 or (epilogue == "residual" and out_dtype == F32)
    outs = pl.pallas_call(
        functools.partial(_mm_kernel, nk=nk, epilogue=epilogue, cast=cast is not None),
        grid=(gm, gn, nk),
        in_specs=in_specs,
        out_specs=out_specs,
        out_shape=out_shape,
        compiler_params=_cparams("parallel", "parallel", "arbitrary"),
        name=name,
    )(*args)
    return outs if cast is not None else outs[0]


def _outproj_kernel(a1_ref, a2_ref, b1_ref, b2_ref, r_ref, o_ref):
    acc = jnp.dot(a1_ref[...], b1_ref[...], preferred_element_type=F32)
    acc = acc + jnp.dot(a2_ref[...], b2_ref[...], preferred_element_type=F32)
    o_ref[...] = r_ref[...] + acc


def outproj(a1, a2, w, layer, res, *, tm=1024, tn=1024):
    m, k1 = a1.shape
    _, k2 = a2.shape
    _, _, n = w.shape
    assert k1 == k2
    tm, tn = _tile(m, tm), _tile(n, tn)
    return pl.pallas_call(
        _outproj_kernel,
        grid=(m // tm, n // tn),
        in_specs=[pl.BlockSpec((tm, k1), lambda i, j: (i, 0)),
                  pl.BlockSpec((tm, k2), lambda i, j: (i, 0)),
                  pl.BlockSpec((None, k1, tn), lambda i, j: (layer, 0, j)),
                  pl.BlockSpec((None, k2, tn), lambda i, j: (layer, 1, j)),
                  pl.BlockSpec((tm, tn), lambda i, j: (i, j))],
        out_specs=pl.BlockSpec((tm, tn), lambda i, j: (i, j)),
        out_shape=jax.ShapeDtypeStruct((m, n), F32),
        compiler_params=_cparams("parallel", "parallel"),
        name="outproj",
    )(a1, a2, w, w, res)


def _glu_kernel(y_ref, w_ref, b_ref, *rest, cast):
    if cast:
        c_ref, o_ref, co_ref = rest
        co_ref[...] = c_ref[...].astype(co_ref.dtype)
    else:
        (o_ref,) = rest
    y = y_ref[...]
    z = jnp.dot(y, w_ref[...], preferred_element_type=F32) + b_ref[...]
    o_ref[...] = (y.astype(F32) * jax.nn.sigmoid(z)).astype(o_ref.dtype)


def glu(y, w, layer, b, *, cast=None, tm=512):
    m, n = y.shape
    tm = _tile(m, tm)
    steps = m // tm
    in_specs = [pl.BlockSpec((tm, n), lambda i: (i, 0)),
                pl.BlockSpec((None, n, n), lambda i: (layer, 0, 0)),
                pl.BlockSpec((1, n), lambda i: (0, 0))]
    args = [y, w, b.reshape(1, n).astype(F32)]
    out_specs = [pl.BlockSpec((tm, n), lambda i: (i, 0))]
    out_shape = [jax.ShapeDtypeStruct((m, n), BF16)]
    if cast is not None:
        w_src, src_layer = cast
        _, ck, cn = w_src.shape
        rows = ck // steps
        assert rows * steps == ck and rows % BF16_SUBLANES == 0
        in_specs.append(pl.BlockSpec((None, rows, cn), lambda i: (src_layer, i, 0)))
        args.append(w_src)
        out_specs.append(pl.BlockSpec((rows, cn), lambda i: (i, 0)))
        out_shape.append(jax.ShapeDtypeStruct((ck, cn), y.dtype))
    outs = pl.pallas_call(
        functools.partial(_glu_kernel, cast=cast is not None),
        grid=(steps,),
        in_specs=in_specs,
        out_specs=out_specs,
        out_shape=out_shape,
        compiler_params=_cparams("parallel"),
        name="glu",
    )(*args)
    return outs if cast is not None else outs[0]


def _t5_bucket_np(n, num_buckets):
    n = np.maximum(n, 0)
    nf = np.maximum(n, 1).astype(np.float32)
    large = MAX_EXACT + (np.log(nf / np.float32(MAX_EXACT)) / np.float32(math.log(MAX_DISTANCE / MAX_EXACT))
                         * np.float32(num_buckets - MAX_EXACT)).astype(np.int32)
    large = np.minimum(large, num_buckets - 1)
    return np.where(n < MAX_EXACT, n, large).astype(np.int32)


def _bias_tiles_kernel(rb_ref, idx_ref, o_ref, *, num_buckets):
    h = pl.program_id(0)
    for d in range(idx_ref.shape[0]):
        idx = idx_ref[d]
        acc = jnp.full(idx.shape, NEG_BIG, F32)
        for k in range(num_buckets):
            acc = jnp.where(idx == k, rb_ref[k, h] * LOG2E, acc)
        o_ref[0, d] = acc


def bias_tiles(rel_bias, t, seq):
    num_buckets, n_heads = rel_bias.shape
    c = np.arange(t)[:, None]
    r = np.arange(t)[None, :]
    far = _t5_bucket_np(np.arange(t + 1, max(seq, t + 2)), num_buckets)
    assert np.all(far == far[0]), "kv tiles two or more steps left of the diagonal must share one bucket"
    idx_far = np.full((t, t), far[0])
    idx_left = _t5_bucket_np(t + r - c, num_buckets)
    idx_diag = np.where(r - c >= 0, _t5_bucket_np(r - c, num_buckets), -1)
    idx = jnp.asarray(np.stack([idx_far, idx_left, idx_diag]).astype(np.int32))
    return pl.pallas_call(
        functools.partial(_bias_tiles_kernel, num_buckets=num_buckets),
        grid=(n_heads,),
        in_specs=[pl.BlockSpec(memory_space=pltpu.SMEM),
                  pl.BlockSpec((3, t, t), lambda h: (0, 0, 0))],
        out_specs=pl.BlockSpec((1, 3, t, t), lambda h: (h, 0, 0, 0)),
        out_shape=jax.ShapeDtypeStruct((n_heads, 3, t, t), F32),
        compiler_params=_cparams("arbitrary"),
        name="bias_tiles",
    )(rel_bias.astype(F32), idx)


def _attn_kernel(lamv_ref, gain_ref, bias_ref, q_ref, k_ref, v_ref, o_ref,
                 vt_ref, qt_ref, sa_ref, sb_ref, mxa_ref, mxb_ref, m_ref, l_ref, acc_ref, *, t, lam_init):
    qi = pl.program_id(2)
    d = DIFF_HEAD_DIM
    hd = 2 * d
    nh, nkv = vt_ref.shape[0], vt_ref.shape[1]
    heads = range(nh)

    def lanes(e):
        return slice(e * hd, (e + 1) * hd)

    @pl.when(qi == 0)
    def _():
        for e in heads:
            for c in range(nkv):
                vt_ref[e, c] = v_ref[c * t:(c + 1) * t, lanes(e)].astype(F32).T.astype(vt_ref.dtype)

    row = lax.broadcasted_iota(jnp.int32, (hd, t), 0)
    for e in heads:
        qT = (q_ref[:, lanes(e)].astype(F32) * (d ** -0.5 * LOG2E)).T
        qt_ref[e, :, 0:t] = jnp.where(row < d, qT, 0.0).astype(qt_ref.dtype)
        qt_ref[e, :, t:2 * t] = jnp.where(row >= d, qT, 0.0).astype(qt_ref.dtype)

    m_ref[...] = jnp.full(m_ref.shape, NEG_BIG, F32)
    l_ref[...] = jnp.zeros(l_ref.shape, F32)
    acc_ref[...] = jnp.zeros(acc_ref.shape, F32)

    buf_a = (sa_ref, mxa_ref)
    buf_b = (sb_ref, mxb_ref)

    def scores(e, j, buf):
        s_ref, mx_ref = buf
        kblk = k_ref[pl.ds(pl.multiple_of(j * t, t), t), lanes(e)]
        s = jnp.dot(kblk, qt_ref[e], preferred_element_type=F32)
        b = bias_ref[e, jnp.clip(j - qi + 2, 0, 2)]
        for c in range(2):
            sc = s[:, c * t:(c + 1) * t] + b
            s_ref[e, :, c * t:(c + 1) * t] = sc
            mx_ref[e, :, c * t:(c + 1) * t] = jnp.max(sc, axis=0, keepdims=True)

    def reduce(e, buf):
        s_ref, mx_ref = buf
        m_prev = m_ref[e]
        m_new = jnp.maximum(m_prev, mx_ref[e])
        alpha = jnp.exp2(m_prev - m_new)
        p = jnp.exp2(s_ref[e] - m_new)
        l_ref[e] = alpha * l_ref[e] + jnp.sum(p, axis=0, keepdims=True)
        m_ref[e] = m_new
        return alpha, p.astype(vt_ref.dtype)

    def accumulate(e, j, alpha, p):
        acc_ref[e] = alpha * acc_ref[e] + jnp.dot(vt_ref[e, j], p, preferred_element_type=F32)

    def step(j, cur, nxt):
        ap = [reduce(e, cur) for e in heads]
        for e in heads:
            scores(e, j + 1, nxt)
        for e in heads:
            accumulate(e, j, *ap[e])

    odd = (qi & 1) == 1

    @pl.when(jnp.logical_not(odd))
    def _():
        for e in heads:
            scores(e, 0, buf_a)

    @pl.when(odd)
    def _():
        for e in heads:
            scores(e, 0, buf_b)
        step(0, buf_b, buf_a)

    def pair(pp, carry):
        j = (qi & 1) + 2 * pp
        step(j, buf_a, buf_b)
        step(j + 1, buf_b, buf_a)
        return carry

    lax.fori_loop(0, lax.shift_right_logical(qi, 1), pair, 0)
    ap = [reduce(e, buf_a) for e in heads]
    for e in heads:
        accumulate(e, qi, *ap[e])

    lv = lamv_ref[...]
    lam = (jnp.exp(jnp.sum(lv[0:1] * lv[1:2], axis=-1, keepdims=True))
           - jnp.exp(jnp.sum(lv[2:3] * lv[3:4], axis=-1, keepdims=True)) + lam_init)
    for e in heads:
        on = acc_ref[e] * (1.0 / l_ref[e])
        o = on[:, 0:t] - lam * on[:, t:2 * t]
        o = o * lax.rsqrt(jnp.mean(o * o, axis=0, keepdims=True) + EPS)
        o = (o * gain_ref[...]) * (1.0 - lam_init)
        o_ref[:, lanes(e)] = o.T.astype(o_ref.dtype)


def diff_attention(qkv, bias, rel_bias, lamv, gain, *, batch, seq, lam_init, t):
    n_heads = rel_bias.shape[1]
    hd = 2 * DIFF_HEAD_DIM
    nq = seq // t
    nh = 2 if n_heads % 2 == 0 else 1
    ng = n_heads // nh
    kernel = functools.partial(_attn_kernel, t=t, lam_init=lam_init)
    stat = pltpu.VMEM((nh, 1, 2 * t), F32)
    tile = pltpu.VMEM((nh, t, 2 * t), F32)
    return pl.pallas_call(
        kernel,
        grid=(batch, ng, nq),
        in_specs=[pl.BlockSpec((4, DIFF_HEAD_DIM), lambda b, h, i: (0, 0)),
                  pl.BlockSpec((hd, 1), lambda b, h, i: (0, 0)),
                  pl.BlockSpec((nh, 3, t, t), lambda b, h, i: (h, 0, 0, 0)),
                  pl.BlockSpec((t, nh * hd), lambda b, h, i: (b * nq + i, h)),
                  pl.BlockSpec((seq, nh * hd), lambda b, h, i: (b, ng + h)),
                  pl.BlockSpec((seq, nh * hd), lambda b, h, i: (b, 2 * ng + h))],
        out_specs=pl.BlockSpec((t, nh * hd), lambda b, h, i: (b * nq + i, h)),
        out_shape=jax.ShapeDtypeStruct((batch * seq, n_heads * hd), BF16),
        scratch_shapes=[pltpu.VMEM((nh, nq, hd, t), BF16), pltpu.VMEM((nh, hd, 2 * t), BF16),
                        tile, tile, stat, stat, stat, stat,
                        pltpu.VMEM((nh, hd, 2 * t), F32)],
        compiler_params=_cparams("parallel", "parallel", "arbitrary"),
        name="diff_attention",
    )(lamv, gain.reshape(hd, 1).astype(F32), bias, qkv, qkv, qkv)


def _s5_prep_kernel(lre_ref, lim_ref, ldt_ref, btr_ref, bti_ref, cr_ref, ci_ref,
                    tg_ref, wg_ref, vgt_ref, ap_ref, e_ref, q_ref, *, gb, chunk, nlev):
    L = chunk
    P = lre_ref.shape[-1]
    W = SSM_GROUP * L
    rows = e_ref.shape[1]
    lane = lax.broadcasted_iota(jnp.int32, (1, 2 * P), 1)
    sgn = jnp.where(lane < P, -1.0, 1.0).astype(F32)
    jj = lax.broadcasted_iota(jnp.int32, (rows, 2 * P), 0).astype(F32)

    def dup(x):
        return jnp.concatenate([x, x], axis=-1)

    def group(g, carry):
        dt = jnp.exp(ldt_ref[g])
        lr, li = lre_ref[g], lim_ref[g]
        lrdt, lidt = lr * dt, li * dt
        mag = jnp.exp(lrdt)
        ar, ai = mag * jnp.cos(lidt), mag * jnp.sin(lidt)
        den = lr * lr + li * li
        nr, ni = ar - 1.0, ai
        gr = (nr * lr + ni * li) / den
        gi = (ni * lr - nr * li) / den
        btr, bti = btr_ref[g], bti_ref[g]
        bbr = gr * btr - gi * bti
        bbi = gr * bti + gi * btr
        cr, ci = cr_ref[g], ci_ref[g]
        bc1 = jnp.concatenate([bbr, bbi], axis=-1)
        bc2 = jnp.concatenate([-bbi, bbr], axis=-1)
        cc1 = jnp.concatenate([cr, -ci], axis=-1)
        cc2 = jnp.concatenate([-ci, -cr], axis=-1)

        mg = jnp.exp(jj * dup(lrdt))
        ang = jj * dup(lidt)
        e_ref[0] = mg * jnp.cos(ang)
        e_ref[1] = mg * jnp.sin(ang)

        for j in range(L + 1):
            er = jnp.broadcast_to(e_ref[0, j:j + 1, :], (SSM_GROUP, 2 * P))
            ei = jnp.broadcast_to(e_ref[1, j:j + 1, :], (SSM_GROUP, 2 * P))
            q_ref[j * SSM_GROUP:(j + 1) * SSM_GROUP, :] = er * cc1 + ei * cc2
            if j < L:
                tau = L - 1 - j
                wg_ref[g, tau * SSM_GROUP:(tau + 1) * SSM_GROUP, :] = (er * bc1 + ei * bc2).astype(wg_ref.dtype)

        vgt_ref[g] = q_ref[SSM_GROUP:SSM_GROUP * (L + 1), :].astype(vgt_ref.dtype)
        strip = lax.dot_general(bc1, q_ref[0:W, :], (((1,), (1,)), ((), ())),
                                precision=lax.Precision.HIGHEST, preferred_element_type=F32)
        pad = jnp.concatenate([jnp.zeros_like(strip), strip], axis=-1)
        lanes_per_step = LANES // SSM_GROUP
        for r in range(lanes_per_step):
            rolled = pad if r == 0 else pltpu.roll(pad, SSM_GROUP * r, axis=1)
            for qd in range(L // lanes_per_step):
                tau = lanes_per_step * qd + r
                tg_ref[g, tau * SSM_GROUP:(tau + 1) * SSM_GROUP, :] = (
                    rolled[:, W - LANES * qd:2 * W - LANES * qd].astype(tg_ref.dtype))

        pr, pi = e_ref[0, L:L + 1, :], e_ref[1, L:L + 1, :]
        for k in range(nlev):
            ap_ref[g, k, 0:1, :] = pr
            ap_ref[g, k, 1:2, :] = pi * sgn
            pr, pi = pr * pr - pi * pi, 2.0 * pr * pi
        return carry

    lax.fori_loop(0, gb, group, 0)


def s5_prep(lam_re, lam_im, log_dt, bt_re, bt_im, c_re, c_im, *, chunk, nlev, gb=8):
    G, P = lam_re.shape
    H = SSM_GROUP
    W = H * chunk
    assert 2 * P == LANES and chunk % (LANES // H) == 0
    gb = _tile(G, gb)
    rows = -(-(chunk + 1) // 8) * 8
    vec = lambda a: a.reshape(G, 1, -1).astype(F32)
    g3 = lambda s1, s2: pl.BlockSpec((gb, s1, s2), lambda i: (i, 0, 0))
    return pl.pallas_call(
        functools.partial(_s5_prep_kernel, gb=gb, chunk=chunk, nlev=nlev),
        grid=(G // gb,),
        in_specs=[g3(1, P), g3(1, P), g3(1, 1), g3(H, P), g3(H, P), g3(H, P), g3(H, P)],
        out_specs=[g3(W, W), g3(W, 2 * P), g3(W, 2 * P),
                   pl.BlockSpec((gb, nlev, 2, 2 * P), lambda i: (i, 0, 0, 0))],
        out_shape=[jax.ShapeDtypeStruct((G, W, W), BF16),
                   jax.ShapeDtypeStruct((G, W, 2 * P), BF16),
                   jax.ShapeDtypeStruct((G, W, 2 * P), BF16),
                   jax.ShapeDtypeStruct((G, nlev, 2, 2 * P), F32)],
        scratch_shapes=[pltpu.VMEM((2, rows, 2 * P), F32),
                        pltpu.VMEM((H * (chunk + 1), 2 * P), F32)],
        compiler_params=_cparams("parallel"),
        name="s5_prep",
    )(vec(lam_re), vec(lam_im), vec(log_dt), bt_re.astype(F32), bt_im.astype(F32),
      c_re.astype(F32), c_im.astype(F32))


def _s5_kernel(x_ref, perm_ref, tg_ref, wg_ref, vgt_ref, ap_ref, d_ref, o_ref, xs_ref, ys_ref,
               *, gb, nc, nlev, na):
    n = x_ref.shape[1]
    P2 = wg_ref.shape[-1]
    cidx = lax.broadcasted_iota(jnp.int32, (n, P2), 0) % nc

    def shift_rows(s, sh):
        return jnp.where(cidx >= sh, pltpu.roll(s, sh, axis=0), 0.0)

    for a in range(na):
        seg = jnp.concatenate([x_ref[OCTET * a + b] for b in range(OCTET)], axis=-1)
        seg = jnp.dot(seg, perm_ref[...], preferred_element_type=F32).astype(xs_ref.dtype)
        for g in range(gb):
            xs_ref[g, a] = seg[:, g * LANES:(g + 1) * LANES]

    def groups(gg, carry):
        gs = [gg * GROUPS_PER_ITER + u for u in range(GROUPS_PER_ITER)]
        xb = [jnp.concatenate([xs_ref[g, a] for a in range(na)], axis=-1) for g in gs]
        s = [jnp.dot(x, wg_ref[g], preferred_element_type=F32) for x, g in zip(xb, gs)]
        for k in range(nlev):
            sh = [shift_rows(v, 1 << k) for v in s]
            s = [v + w * ap_ref[g, k, 0:1, :] + pltpu.roll(w, P2 // 2, axis=1) * ap_ref[g, k, 1:2, :]
                 for v, w, g in zip(s, sh, gs)]
        for x, v, g in zip(xb, s, gs):
            s_in = shift_rows(v, 1)
            y = jnp.dot(x, tg_ref[g], preferred_element_type=F32)
            y = y + lax.dot_general(s_in.astype(BF16), vgt_ref[g], (((1,), (1,)), ((), ())),
                                    preferred_element_type=F32)
            y = y + x.astype(F32) * d_ref[g]
            yb = jax.nn.gelu(y).astype(ys_ref.dtype)
            for a in range(na):
                ys_ref[a, g] = yb[:, a * LANES:(a + 1) * LANES]
        return carry

    lax.fori_loop(0, gb // GROUPS_PER_ITER, groups, 0)

    for a in range(na):
        seg = jnp.concatenate([ys_ref[a, g] for g in range(gb)], axis=-1)
        seg = lax.dot_general(seg, perm_ref[...], (((1,), (1,)), ((), ())), preferred_element_type=F32)
        for b in range(OCTET):
            o_ref[OCTET * a + b] = seg[:, b * LANES:(b + 1) * LANES].astype(o_ref.dtype)


def s5_apply(ut, tg, wg, vgt, ap, dt, *, nc, nlev):
    L, n, width = ut.shape
    G = width // SSM_GROUP
    W = SSM_GROUP * L
    gb = LANES // SSM_GROUP
    assert G % gb == 0 and L % OCTET == 0 and gb * SSM_GROUP * OCTET == OCTET * LANES
    lane = np.arange(OCTET * LANES)
    b, g, h = lane // LANES, (lane % LANES) // SSM_GROUP, lane % SSM_GROUP
    perm = np.zeros((OCTET * LANES, OCTET * LANES), np.float32)
    perm[lane, g * (OCTET * SSM_GROUP) + b * SSM_GROUP + h] = 1.0
    g3 = lambda s1, s2: pl.BlockSpec((gb, s1, s2), lambda i: (i, 0, 0))
    io = pl.BlockSpec((L, n, LANES), lambda i: (0, 0, i))
    return pl.pallas_call(
        functools.partial(_s5_kernel, gb=gb, nc=nc, nlev=nlev, na=L // OCTET),
        grid=(G // gb,),
        in_specs=[io, pl.BlockSpec(perm.shape, lambda i: (0, 0)), g3(W, W), g3(W, LANES), g3(W, LANES),
                  pl.BlockSpec((gb, nlev, 2, LANES), lambda i: (i, 0, 0, 0)), g3(1, W)],
        out_specs=io,
        out_shape=jax.ShapeDtypeStruct((L, n, width), BF16),
        scratch_shapes=[pltpu.VMEM((gb, L // OCTET, n, LANES), BF16),
                        pltpu.VMEM((L // OCTET, gb, n, LANES), BF16)],
        compiler_params=_cparams("parallel"),
        name="s5_apply",
    )(ut, jnp.asarray(perm, BF16), tg, wg, vgt, ap, dt)


def s5_layer(u, lam_re, lam_im, log_dt, b_re, b_im, c_re, c_im, d, *, batch, seq):
    T, width = u.shape
    G = width // SSM_GROUP
    L = _tile(seq, SSM_CHUNK)
    nc = seq // L
    nlev = max(nc - 1, 0).bit_length()
    n = batch * nc
    W = SSM_GROUP * L
    bt_re = jnp.swapaxes(b_re, -1, -2)
    bt_im = jnp.swapaxes(b_im, -1, -2)
    tg, wg, vgt, ap = s5_prep(lam_re, lam_im, log_dt, bt_re, bt_im, c_re, c_im, chunk=L, nlev=max(nlev, 1))
    ut = u.reshape(n, L, width).swapaxes(0, 1)
    dtile = jnp.tile(d.astype(F32), (1, L)).reshape(G, 1, W)
    yt = s5_apply(ut, tg, wg, vgt, ap, dtile, nc=nc, nlev=nlev)
    return yt.swapaxes(0, 1).reshape(T, width)


@jax.jit
def _forward(x, w_in, w_out, norm_mix, norm_mlp, w_up, w_down, norm_final, rel_bias,
             lambda_q1, lambda_k1, lambda_q2, lambda_k2, subln_gain,
             ssm_lambda_re, ssm_lambda_im, ssm_log_dt, ssm_b_re, ssm_b_im,
             ssm_c_re, ssm_c_im, ssm_d, w_glu, b_glu):
    batch, seq, d_model = x.shape
    depth = w_in.shape[0]
    ssm_width = w_glu.shape[-1]
    attn_width = (w_in.shape[-1] - ssm_width) // 3
    T = batch * seq

    t_attn = _tile(seq, 512)
    bias = bias_tiles(rel_bias, t_attn, seq)

    w_out_b, w_glu_b, w_in_b = w_out.astype(BF16), w_glu.astype(BF16), w_in[0].astype(BF16)
    h = x.reshape(T, d_model).astype(F32)
    for i in range(depth):
        lam_init = 0.8 - 0.6 * math.exp(-0.3 * i)
        xn = rmsnorm(h, norm_mix[i], BF16)
        proj, w_up_b = matmul(xn, w_in_b, out_dtype=BF16, cast=(w_up, i), name="proj_in")
        lamv = jnp.stack([lambda_q1[i], lambda_k1[i], lambda_q2[i], lambda_k2[i]]).astype(F32)
        attn = diff_attention(proj, bias, rel_bias, lamv, subln_gain[i],
                              batch=batch, seq=seq, lam_init=lam_init, t=t_attn)
        y = s5_layer(proj[:, 3 * attn_width:], ssm_lambda_re[i], ssm_lambda_im[i], ssm_log_dt[i],
                     ssm_b_re[i], ssm_b_im[i], ssm_c_re[i], ssm_c_im[i], ssm_d[i], batch=batch, seq=seq)
        if i + 1 < depth:
            ssm, w_in_b = glu(y, w_glu_b, i, b_glu[i], cast=(w_in, i + 1))
        else:
            ssm = glu(y, w_glu_b, i, b_glu[i])
        h = outproj(attn, ssm, w_out_b, i, h)
        xn = rmsnorm(h, norm_mlp[i], BF16)
        ff, w_down_b = matmul(xn, w_up_b, out_dtype=BF16, epilogue="relu2", cast=(w_down, i), name="mlp_up")
        h = matmul(ff, w_down_b, out_dtype=F32, epilogue="residual", res=h, tk=4096, name="mlp_down")
    out = rmsnorm(h, norm_final, x.dtype)
    return out.reshape(batch, seq, d_model)


def kernel(x, w_in, w_out, norm_mix, norm_mlp, w_up, w_down, norm_final, rel_bias, lambda_q1, lambda_k1, lambda_q2, lambda_k2, subln_gain, ssm_lambda_re, ssm_lambda_im, ssm_log_dt, ssm_b_re, ssm_b_im, ssm_c_re, ssm_c_im, ssm_d, w_glu, b_glu):
    return _forward(x, w_in, w_out, norm_mix, norm_mlp, w_up, w_down, norm_final, rel_bias,
                    lambda_q1, lambda_k1, lambda_q2, lambda_k2, subln_gain,
                    ssm_lambda_re, ssm_lambda_im, ssm_log_dt, ssm_b_re, ssm_b_im,
                    ssm_c_re, ssm_c_im, ssm_d, w_glu, b_glu)
```

```python
import functools
import math

import numpy as np
import jax
import jax.numpy as jnp
from jax import lax
from jax.experimental import pallas as pl
from jax.experimental.pallas import tpu as pltpu

F32 = jnp.float32
BF16 = jnp.bfloat16

EPS = 1e-6
MAX_EXACT = 16
MAX_DISTANCE = 128
DIFF_HEAD_DIM = 64
SSM_GROUP = 16
SSM_CHUNK = 32
OCTET = 8
GROUPS_PER_ITER = 4
NEG_BIG = -1e30
LOG2E = math.log2(math.e)

V7X_VMEM_LIMIT_BYTES = 56 * 1024 * 1024
LANES = 128
BF16_SUBLANES = 16


def _cparams(*sem):
    return pltpu.CompilerParams(dimension_semantics=sem, vmem_limit_bytes=V7X_VMEM_LIMIT_BYTES)


def _tile(dim, pref):
    t = min(dim, pref)
    while dim % t:
        t //= 2
    return t


def _rmsnorm_kernel(x_ref, g_ref, o_ref):
    x = x_ref[...].astype(F32)
    ms = jnp.mean(x * x, axis=-1, keepdims=True)
    o_ref[...] = ((x * lax.rsqrt(ms + EPS)) * g_ref[...]).astype(o_ref.dtype)


def rmsnorm(x, g, out_dtype):
    m, d = x.shape
    tm = _tile(m, 256)
    return pl.pallas_call(
        _rmsnorm_kernel,
        grid=(m // tm,),
        in_specs=[pl.BlockSpec((tm, d), lambda i: (i, 0)),
                  pl.BlockSpec((1, d), lambda i: (0, 0))],
        out_specs=pl.BlockSpec((tm, d), lambda i: (i, 0)),
        out_shape=jax.ShapeDtypeStruct((m, d), out_dtype),
        compiler_params=_cparams("parallel"),
        name="rmsnorm",
    )(x, g.reshape(1, d).astype(F32))


def _mm_kernel(a_ref, b_ref, *rest, nk, epilogue, cast, mean_dim):
    rest = list(rest)
    r_ref = rest.pop(0) if epilogue == "residual" else None
    ssq_ref = rest.pop(0) if mean_dim else None
    if cast:
        c_ref, o_ref, co_ref = rest
        co_ref[...] = c_ref[...].astype(co_ref.dtype)
    else:
        (o_ref,) = rest

    if nk == 1:
        acc = jnp.dot(a_ref[...], b_ref[...], preferred_element_type=F32)
        if mean_dim:
            ms = jnp.sum(ssq_ref[...], axis=-1, keepdims=True) * (1.0 / mean_dim)
            acc = acc * lax.rsqrt(ms + EPS)
        if epilogue == "relu2":
            acc = jnp.square(jnp.maximum(acc, 0.0))
        elif epilogue == "residual":
            acc = r_ref[...] + acc
        o_ref[...] = acc.astype(o_ref.dtype)
        return

    @pl.when(pl.program_id(2) == 0)
    def _():
        o_ref[...] = r_ref[...]

    o_ref[...] += jnp.dot(a_ref[...], b_ref[...], preferred_element_type=F32)


def matmul(a, w, *, out_dtype, epilogue="none", res=None, cast=None, row_ssq=None,
           tm=1024, tn=1024, tk=4096, name="matmul"):
    m, kdim = a.shape
    _, n = w.shape
    tm, tn, tk = _tile(m, tm), _tile(n, tn), _tile(kdim, tk)
    gm, gn, nk = m // tm, n // tn, kdim // tk
    in_specs = [pl.BlockSpec((tm, tk), lambda i, j, k: (i, k)),
                pl.BlockSpec((tk, tn), lambda i, j, k: (k, j))]
    args = [a, w]
    if epilogue == "residual":
        in_specs.append(pl.BlockSpec((tm, tn), lambda i, j, k: (i, j)))
        args.append(res)
    if row_ssq is not None:
        assert nk == 1
        in_specs.append(pl.BlockSpec((tm, LANES), lambda i, j, k: (i, 0)))
        args.append(row_ssq)
    out_specs = [pl.BlockSpec((tm, tn), lambda i, j, k: (i, j))]
    out_shape = [jax.ShapeDtypeStruct((m, n), out_dtype)]
    if cast is not None:
        w_src, layer = cast
        _, ck, cn = w_src.shape
        rows = ck // (gm * gn * nk)
        assert rows * gm * gn * nk == ck and rows % BF16_SUBLANES == 0
        step = lambda i, j, k: (i * gn + j) * nk + k
        in_specs.append(pl.BlockSpec((None, rows, cn), lambda i, j, k: (layer, step(i, j, k), 0)))
        args.append(w_src)
        out_specs.append(pl.BlockSpec((rows, cn), lambda i, j, k: (step(i, j, k), 0)))
        out_shape.append(jax.ShapeDtypeStruct((ck, cn), a.dtype))
    assert nk == 1 or (epilogue == "residual" and out_dtype == F32)
    outs = pl.pallas_call(
        functools.partial(_mm_kernel, nk=nk, epilogue=epilogue, cast=cast is not None,
                          mean_dim=kdim if row_ssq is not None else 0),
        grid=(gm, gn, nk),
        in_specs=in_specs,
        out_specs=out_specs,
        out_shape=out_shape,
        compiler_params=_cparams("parallel", "parallel", "arbitrary"),
        name=name,
    )(*args)
    return outs if cast is not None else outs[0]


def _outproj_kernel(a1_ref, a2_ref, b1_ref, b2_ref, r_ref, g_ref, o_ref, hg_ref, ssq_ref):
    acc = jnp.dot(a1_ref[...], b1_ref[...], preferred_element_type=F32)
    acc = acc + jnp.dot(a2_ref[...], b2_ref[...], preferred_element_type=F32)
    h = r_ref[...] + acc
    o_ref[...] = h
    hg_ref[...] = (h * g_ref[...]).astype(hg_ref.dtype)
    h2 = h * h
    part = h2[:, 0:LANES]
    for c in range(1, h2.shape[1] // LANES):
        part = part + h2[:, c * LANES:(c + 1) * LANES]

    @pl.when(pl.program_id(1) == 0)
    def _():
        ssq_ref[...] = part

    @pl.when(pl.program_id(1) > 0)
    def _():
        ssq_ref[...] += part


def outproj(a1, a2, w, layer, res, gain, *, tm=1024, tn=512):
    m, k1 = a1.shape
    _, k2 = a2.shape
    _, _, n = w.shape
    assert k1 == k2
    tm, tn = _tile(m, tm), _tile(n, tn)
    return pl.pallas_call(
        _outproj_kernel,
        grid=(m // tm, n // tn),
        in_specs=[pl.BlockSpec((tm, k1), lambda i, j: (i, 0)),
                  pl.BlockSpec((tm, k2), lambda i, j: (i, 0)),
                  pl.BlockSpec((None, k1, tn), lambda i, j: (layer, 0, j)),
                  pl.BlockSpec((None, k2, tn), lambda i, j: (layer, 1, j)),
                  pl.BlockSpec((tm, tn), lambda i, j: (i, j)),
                  pl.BlockSpec((1, tn), lambda i, j: (0, j))],
        out_specs=[pl.BlockSpec((tm, tn), lambda i, j: (i, j)),
                   pl.BlockSpec((tm, tn), lambda i, j: (i, j)),
                   pl.BlockSpec((tm, LANES), lambda i, j: (i, 0))],
        out_shape=[jax.ShapeDtypeStruct((m, n), F32),
                   jax.ShapeDtypeStruct((m, n), a1.dtype),
                   jax.ShapeDtypeStruct((m, LANES), F32)],
        compiler_params=_cparams("parallel", "arbitrary"),
        name="outproj",
    )(a1, a2, w, w, res, gain.reshape(1, n).astype(F32))


def _glu_kernel(y_ref, w_ref, b_ref, *rest, cast):
    if cast:
        c_ref, o_ref, co_ref = rest
        co_ref[...] = c_ref[...].astype(co_ref.dtype)
    else:
        (o_ref,) = rest
    y = y_ref[...]
    z = jnp.dot(y, w_ref[...], preferred_element_type=F32) + b_ref[...]
    o_ref[...] = (y.astype(F32) * jax.nn.sigmoid(z)).astype(o_ref.dtype)


def glu(y, w, layer, b, *, cast=None, tm=512):
    m, n = y.shape
    tm = _tile(m, tm)
    steps = m // tm
    in_specs = [pl.BlockSpec((tm, n), lambda i: (i, 0)),
                pl.BlockSpec((None, n, n), lambda i: (layer, 0, 0)),
                pl.BlockSpec((1, n), lambda i: (0, 0))]
    args = [y, w, b.reshape(1, n).astype(F32)]
    out_specs = [pl.BlockSpec((tm, n), lambda i: (i, 0))]
    out_shape = [jax.ShapeDtypeStruct((m, n), BF16)]
    if cast is not None:
        w_src, src_layer = cast
        _, ck, cn = w_src.shape
        rows = ck // steps
        assert rows * steps == ck and rows % BF16_SUBLANES == 0
        in_specs.append(pl.BlockSpec((None, rows, cn), lambda i: (src_layer, i, 0)))
        args.append(w_src)
        out_specs.append(pl.BlockSpec((rows, cn), lambda i: (i, 0)))
        out_shape.append(jax.ShapeDtypeStruct((ck, cn), y.dtype))
    outs = pl.pallas_call(
        functools.partial(_glu_kernel, cast=cast is not None),
        grid=(steps,),
        in_specs=in_specs,
        out_specs=out_specs,
        out_shape=out_shape,
        compiler_params=_cparams("parallel"),
        name="glu",
    )(*args)
    return outs if cast is not None else outs[0]


def _t5_bucket_np(n, num_buckets):
    n = np.maximum(n, 0)
    nf = np.maximum(n, 1).astype(np.float32)
    large = MAX_EXACT + (np.log(nf / np.float32(MAX_EXACT)) / np.float32(math.log(MAX_DISTANCE / MAX_EXACT))
                         * np.float32(num_buckets - MAX_EXACT)).astype(np.int32)
    large = np.minimum(large, num_buckets - 1)
    return np.where(n < MAX_EXACT, n, large).astype(np.int32)


def _bias_tiles_kernel(rb_ref, idx_ref, o_ref, *, num_buckets):
    h = pl.program_id(0)
    for d in range(idx_ref.shape[0]):
        idx = idx_ref[d]
        acc = jnp.full(idx.shape, NEG_BIG, F32)
        for k in range(num_buckets):
            acc = jnp.where(idx == k, rb_ref[k, h] * LOG2E, acc)
        o_ref[0, d] = acc


def bias_tiles(rel_bias, t, seq):
    num_buckets, n_heads = rel_bias.shape
    c = np.arange(t)[:, None]
    r = np.arange(t)[None, :]
    far = _t5_bucket_np(np.arange(t + 1, max(seq, t + 2)), num_buckets)
    assert np.all(far == far[0]), "kv tiles two or more steps left of the diagonal must share one bucket"
    idx_far = np.full((t, t), far[0])
    idx_left = _t5_bucket_np(t + r - c, num_buckets)
    idx_diag = np.where(r - c >= 0, _t5_bucket_np(r - c, num_buckets), -1)
    idx = jnp.asarray(np.stack([idx_far, idx_left, idx_diag]).astype(np.int32))
    return pl.pallas_call(
        functools.partial(_bias_tiles_kernel, num_buckets=num_buckets),
        grid=(n_heads,),
        in_specs=[pl.BlockSpec(memory_space=pltpu.SMEM),
                  pl.BlockSpec((3, t, t), lambda h: (0, 0, 0))],
        out_specs=pl.BlockSpec((1, 3, t, t), lambda h: (h, 0, 0, 0)),
        out_shape=jax.ShapeDtypeStruct((n_heads, 3, t, t), F32),
        compiler_params=_cparams("arbitrary"),
        name="bias_tiles",
    )(rel_bias.astype(F32), idx)


def _attn_kernel(lamv_ref, gain_ref, bias_ref, q_ref, k_ref, v_ref, o_ref,
                 vt_ref, qt_ref, sa_ref, sb_ref, mxa_ref, mxb_ref, m_ref, l_ref, acc_ref, *, t, lam_init):
    qi = pl.program_id(2)
    d = DIFF_HEAD_DIM
    hd = 2 * d
    nh, nkv = vt_ref.shape[0], vt_ref.shape[1]
    heads = range(nh)

    def lanes(e):
        return slice(e * hd, (e + 1) * hd)

    @pl.when(qi == 0)
    def _():
        for e in heads:
            for c in range(nkv):
                vt_ref[e, c] = v_ref[c * t:(c + 1) * t, lanes(e)].astype(F32).T.astype(vt_ref.dtype)

    row = lax.broadcasted_iota(jnp.int32, (hd, t), 0)
    for e in heads:
        qT = (q_ref[:, lanes(e)].astype(F32) * (d ** -0.5 * LOG2E)).T
        qt_ref[e, :, 0:t] = jnp.where(row < d, qT, 0.0).astype(qt_ref.dtype)
        qt_ref[e, :, t:2 * t] = jnp.where(row >= d, qT, 0.0).astype(qt_ref.dtype)

    m_ref[...] = jnp.full(m_ref.shape, NEG_BIG, F32)
    l_ref[...] = jnp.zeros(l_ref.shape, F32)
    acc_ref[...] = jnp.zeros(acc_ref.shape, F32)

    buf_a = (sa_ref, mxa_ref)
    buf_b = (sb_ref, mxb_ref)

    def scores(e, j, buf):
        s_ref, mx_ref = buf
        kblk = k_ref[pl.ds(pl.multiple_of(j * t, t), t), lanes(e)]
        s = jnp.dot(kblk, qt_ref[e], preferred_element_type=F32)
        b = bias_ref[e, jnp.clip(j - qi + 2, 0, 2)]
        for c in range(2):
            sc = s[:, c * t:(c + 1) * t] + b
            s_ref[e, :, c * t:(c + 1) * t] = sc
            mx_ref[e, :, c * t:(c + 1) * t] = jnp.max(sc, axis=0, keepdims=True)

    def reduce(e, buf):
        s_ref, mx_ref = buf
        m_prev = m_ref[e]
        m_new = jnp.maximum(m_prev, mx_ref[e])
        alpha = jnp.exp2(m_prev - m_new)
        p = jnp.exp2(s_ref[e] - m_new)
        l_ref[e] = alpha * l_ref[e] + jnp.sum(p, axis=0, keepdims=True)
        m_ref[e] = m_new
        return alpha, p.astype(vt_ref.dtype)

    def accumulate(e, j, alpha, p):
        acc_ref[e] = alpha * acc_ref[e] + jnp.dot(vt_ref[e, j], p, preferred_element_type=F32)

    def step(j, cur, nxt):
        ap = [reduce(e, cur) for e in heads]
        for e in heads:
            scores(e, j + 1, nxt)
        for e in heads:
            accumulate(e, j, *ap[e])

    odd = (qi & 1) == 1

    @pl.when(jnp.logical_not(odd))
    def _():
        for e in heads:
            scores(e, 0, buf_a)

    @pl.when(odd)
    def _():
        for e in heads:
            scores(e, 0, buf_b)
        step(0, buf_b, buf_a)

    def pair(pp, carry):
        j = (qi & 1) + 2 * pp
        step(j, buf_a, buf_b)
        step(j + 1, buf_b, buf_a)
        return carry

    lax.fori_loop(0, lax.shift_right_logical(qi, 1), pair, 0)
    ap = [reduce(e, buf_a) for e in heads]
    for e in heads:
        accumulate(e, qi, *ap[e])

    lv = lamv_ref[...]
    lam = (jnp.exp(jnp.sum(lv[0:1] * lv[1:2], axis=-1, keepdims=True))
           - jnp.exp(jnp.sum(lv[2:3] * lv[3:4], axis=-1, keepdims=True)) + lam_init)
    for e in heads:
        on = acc_ref[e] * (1.0 / l_ref[e])
        o = on[:, 0:t] - lam * on[:, t:2 * t]
        o = o * lax.rsqrt(jnp.mean(o * o, axis=0, keepdims=True) + EPS)
        o = (o * gain_ref[...]) * (1.0 - lam_init)
        o_ref[:, lanes(e)] = o.T.astype(o_ref.dtype)


def diff_attention(qkv, bias, rel_bias, lamv, gain, *, batch, seq, lam_init, t):
    n_heads = rel_bias.shape[1]
    hd = 2 * DIFF_HEAD_DIM
    nq = seq // t
    nh = 2 if n_heads % 2 == 0 else 1
    ng = n_heads // nh
    kernel = functools.partial(_attn_kernel, t=t, lam_init=lam_init)
    stat = pltpu.VMEM((nh, 1, 2 * t), F32)
    tile = pltpu.VMEM((nh, t, 2 * t), F32)
    return pl.pallas_call(
        kernel,
        grid=(batch, ng, nq),
        in_specs=[pl.BlockSpec((4, DIFF_HEAD_DIM), lambda b, h, i: (0, 0)),
                  pl.BlockSpec((hd, 1), lambda b, h, i: (0, 0)),
                  pl.BlockSpec((nh, 3, t, t), lambda b, h, i: (h, 0, 0, 0)),
                  pl.BlockSpec((t, nh * hd), lambda b, h, i: (b * nq + i, h)),
                  pl.BlockSpec((seq, nh * hd), lambda b, h, i: (b, ng + h)),
                  pl.BlockSpec((seq, nh * hd), lambda b, h, i: (b, 2 * ng + h))],
        out_specs=pl.BlockSpec((t, nh * hd), lambda b, h, i: (b * nq + i, h)),
        out_shape=jax.ShapeDtypeStruct((batch * seq, n_heads * hd), BF16),
        scratch_shapes=[pltpu.VMEM((nh, nq, hd, t), BF16), pltpu.VMEM((nh, hd, 2 * t), BF16),
                        tile, tile, stat, stat, stat, stat,
                        pltpu.VMEM((nh, hd, 2 * t), F32)],
        compiler_params=_cparams("parallel", "parallel", "arbitrary"),
        name="diff_attention",
    )(lamv, gain.reshape(hd, 1).astype(F32), bias, qkv, qkv, qkv)


def _s5_prep_kernel(lre_ref, lim_ref, ldt_ref, btr_ref, bti_ref, cr_ref, ci_ref,
                    tg_ref, wg_ref, vgt_ref, ap_ref, es_ref, qs_ref, *, gb, chunk, nlev):
    L = chunk
    P = lre_ref.shape[-1]
    W = SSM_GROUP * L
    rows = es_ref.shape[2]
    lane = lax.broadcasted_iota(jnp.int32, (1, 2 * P), 1)
    sgn = jnp.where(lane < P, -1.0, 1.0).astype(F32)
    jj = lax.broadcasted_iota(jnp.int32, (rows, 2 * P), 0).astype(F32)

    def dup(x):
        return jnp.concatenate([x, x], axis=-1)

    def group(g, e_ref, q_ref):
        dt = jnp.exp(ldt_ref[g])
        lr, li = lre_ref[g], lim_ref[g]
        lrdt, lidt = lr * dt, li * dt
        mag = jnp.exp(lrdt)
        ar, ai = mag * jnp.cos(lidt), mag * jnp.sin(lidt)
        den = lr * lr + li * li
        nr, ni = ar - 1.0, ai
        gr = (nr * lr + ni * li) / den
        gi = (ni * lr - nr * li) / den
        btr, bti = btr_ref[g], bti_ref[g]
        bbr = gr * btr - gi * bti
        bbi = gr * bti + gi * btr
        cr, ci = cr_ref[g], ci_ref[g]
        bc1 = jnp.concatenate([bbr, bbi], axis=-1)
        bc2 = jnp.concatenate([-bbi, bbr], axis=-1)
        cc1 = jnp.concatenate([cr, -ci], axis=-1)
        cc2 = jnp.concatenate([-ci, -cr], axis=-1)

        mg = jnp.exp(jj * dup(lrdt))
        ang = jj * dup(lidt)
        e_ref[0] = mg * jnp.cos(ang)
        e_ref[1] = mg * jnp.sin(ang)

        for j in range(L + 1):
            er = jnp.broadcast_to(e_ref[0, j:j + 1, :], (SSM_GROUP, 2 * P))
            ei = jnp.broadcast_to(e_ref[1, j:j + 1, :], (SSM_GROUP, 2 * P))
            q_ref[j * SSM_GROUP:(j + 1) * SSM_GROUP, :] = er * cc1 + ei * cc2
            if j < L:
                tau = L - 1 - j
                wg_ref[g, tau * SSM_GROUP:(tau + 1) * SSM_GROUP, :] = (er * bc1 + ei * bc2).astype(wg_ref.dtype)

        vgt_ref[g] = q_ref[SSM_GROUP:SSM_GROUP * (L + 1), :].astype(vgt_ref.dtype)
        strip = lax.dot_general(bc1, q_ref[0:W, :], (((1,), (1,)), ((), ())),
                                precision=lax.Precision.HIGHEST, preferred_element_type=F32)
        pad = jnp.concatenate([jnp.zeros_like(strip), strip], axis=-1)
        lanes_per_step = LANES // SSM_GROUP
        for r in range(lanes_per_step):
            rolled = pad if r == 0 else pltpu.roll(pad, SSM_GROUP * r, axis=1)
            for qd in range(L // lanes_per_step):
                tau = lanes_per_step * qd + r
                tg_ref[g, tau * SSM_GROUP:(tau + 1) * SSM_GROUP, :] = (
                    rolled[:, W - LANES * qd:2 * W - LANES * qd].astype(tg_ref.dtype))

        pr, pi = e_ref[0, L:L + 1, :], e_ref[1, L:L + 1, :]
        for k in range(nlev):
            ap_ref[g, k, 0:1, :] = pr
            ap_ref[g, k, 1:2, :] = pi * sgn
            pr, pi = pr * pr - pi * pi, 2.0 * pr * pi

    def groups(gg, carry):
        for u in range(GROUPS_PER_ITER):
            group(gg * GROUPS_PER_ITER + u, es_ref.at[u], qs_ref.at[u])
        return carry

    lax.fori_loop(0, gb // GROUPS_PER_ITER, groups, 0)


def s5_prep(lam_re, lam_im, log_dt, bt_re, bt_im, c_re, c_im, *, chunk, nlev, gb=8):
    G, P = lam_re.shape
    H = SSM_GROUP
    W = H * chunk
    assert 2 * P == LANES and chunk % (LANES // H) == 0
    gb = _tile(G, gb)
    rows = -(-(chunk + 1) // 8) * 8
    vec = lambda a: a.reshape(G, 1, -1).astype(F32)
    g3 = lambda s1, s2: pl.BlockSpec((gb, s1, s2), lambda i: (i, 0, 0))
    return pl.pallas_call(
        functools.partial(_s5_prep_kernel, gb=gb, chunk=chunk, nlev=nlev),
        grid=(G // gb,),
        in_specs=[g3(1, P), g3(1, P), g3(1, 1), g3(H, P), g3(H, P), g3(H, P), g3(H, P)],
        out_specs=[g3(W, W), g3(W, 2 * P), g3(W, 2 * P),
                   pl.BlockSpec((gb, nlev, 2, 2 * P), lambda i: (i, 0, 0, 0))],
        out_shape=[jax.ShapeDtypeStruct((G, W, W), BF16),
                   jax.ShapeDtypeStruct((G, W, 2 * P), BF16),
                   jax.ShapeDtypeStruct((G, W, 2 * P), BF16),
                   jax.ShapeDtypeStruct((G, nlev, 2, 2 * P), F32)],
        scratch_shapes=[pltpu.VMEM((GROUPS_PER_ITER, 2, rows, 2 * P), F32),
                        pltpu.VMEM((GROUPS_PER_ITER, H * (chunk + 1), 2 * P), F32)],
        compiler_params=_cparams("parallel"),
        name="s5_prep",
    )(vec(lam_re), vec(lam_im), vec(log_dt), bt_re.astype(F32), bt_im.astype(F32),
      c_re.astype(F32), c_im.astype(F32))


def _s5_kernel(x_ref, perm_ref, tg_ref, wg_ref, vgt_ref, ap_ref, d_ref, o_ref, xs_ref, ys_ref,
               *, gb, nc, nlev, na):
    n = x_ref.shape[1]
    P2 = wg_ref.shape[-1]
    cidx = lax.broadcasted_iota(jnp.int32, (n, P2), 0) % nc

    def shift_rows(s, sh):
        return jnp.where(cidx >= sh, pltpu.roll(s, sh, axis=0), 0.0)

    for a in range(na):
        seg = jnp.concatenate([x_ref[OCTET * a + b] for b in range(OCTET)], axis=-1)
        seg = jnp.dot(seg, perm_ref[...], preferred_element_type=F32).astype(xs_ref.dtype)
        for g in range(gb):
            xs_ref[g, a] = seg[:, g * LANES:(g + 1) * LANES]

    def groups(gg, carry):
        gs = [gg * GROUPS_PER_ITER + u for u in range(GROUPS_PER_ITER)]
        xb = [jnp.concatenate([xs_ref[g, a] for a in range(na)], axis=-1) for g in gs]
        s = [jnp.dot(x, wg_ref[g], preferred_element_type=F32) for x, g in zip(xb, gs)]
        for k in range(nlev):
            sh = [shift_rows(v, 1 << k) for v in s]
            s = [v + w * ap_ref[g, k, 0:1, :] + pltpu.roll(w, P2 // 2, axis=1) * ap_ref[g, k, 1:2, :]
                 for v, w, g in zip(s, sh, gs)]
        for x, v, g in zip(xb, s, gs):
            s_in = shift_rows(v, 1)
            y = jnp.dot(x, tg_ref[g], preferred_element_type=F32)
            y = y + lax.dot_general(s_in.astype(BF16), vgt_ref[g], (((1,), (1,)), ((), ())),
                                    preferred_element_type=F32)
            y = y + x.astype(F32) * d_ref[g]
            yb = jax.nn.gelu(y).astype(ys_ref.dtype)
            for a in range(na):
                ys_ref[a, g] = yb[:, a * LANES:(a + 1) * LANES]
        return carry

    lax.fori_loop(0, gb // GROUPS_PER_ITER, groups, 0)

    for a in range(na):
        seg = jnp.concatenate([ys_ref[a, g] for g in range(gb)], axis=-1)
        seg = lax.dot_general(seg, perm_ref[...], (((1,), (1,)), ((), ())), preferred_element_type=F32)
        for b in range(OCTET):
            o_ref[OCTET * a + b] = seg[:, b * LANES:(b + 1) * LANES].astype(o_ref.dtype)


def s5_apply(ut, tg, wg, vgt, ap, dt, *, nc, nlev):
    L, n, width = ut.shape
    G = width // SSM_GROUP
    W = SSM_GROUP * L
    gb = LANES // SSM_GROUP
    assert G % gb == 0 and L % OCTET == 0 and gb * SSM_GROUP * OCTET == OCTET * LANES
    lane = np.arange(OCTET * LANES)
    b, g, h = lane // LANES, (lane % LANES) // SSM_GROUP, lane % SSM_GROUP
    perm = np.zeros((OCTET * LANES, OCTET * LANES), np.float32)
    perm[lane, g * (OCTET * SSM_GROUP) + b * SSM_GROUP + h] = 1.0
    g3 = lambda s1, s2: pl.BlockSpec((gb, s1, s2), lambda i: (i, 0, 0))
    io = pl.BlockSpec((L, n, LANES), lambda i: (0, 0, i))
    return pl.pallas_call(
        functools.partial(_s5_kernel, gb=gb, nc=nc, nlev=nlev, na=L // OCTET),
        grid=(G // gb,),
        in_specs=[io, pl.BlockSpec(perm.shape, lambda i: (0, 0)), g3(W, W), g3(W, LANES), g3(W, LANES),
                  pl.BlockSpec((gb, nlev, 2, LANES), lambda i: (i, 0, 0, 0)), g3(1, W)],
        out_specs=io,
        out_shape=jax.ShapeDtypeStruct((L, n, width), BF16),
        scratch_shapes=[pltpu.VMEM((gb, L // OCTET, n, LANES), BF16),
                        pltpu.VMEM((L // OCTET, gb, n, LANES), BF16)],
        compiler_params=_cparams("parallel"),
        name="s5_apply",
    )(ut, jnp.asarray(perm, BF16), tg, wg, vgt, ap, dt)


def s5_layer(u, lam_re, lam_im, log_dt, b_re, b_im, c_re, c_im, d, *, batch, seq):
    T, width = u.shape
    G = width // SSM_GROUP
    L = _tile(seq, SSM_CHUNK)
    nc = seq // L
    nlev = max(nc - 1, 0).bit_length()
    n = batch * nc
    W = SSM_GROUP * L
    bt_re = jnp.swapaxes(b_re, -1, -2)
    bt_im = jnp.swapaxes(b_im, -1, -2)
    tg, wg, vgt, ap = s5_prep(lam_re, lam_im, log_dt, bt_re, bt_im, c_re, c_im, chunk=L, nlev=max(nlev, 1))
    ut = u.reshape(n, L, width).swapaxes(0, 1)
    dtile = jnp.tile(d.astype(F32), (1, L)).reshape(G, 1, W)
    yt = s5_apply(ut, tg, wg, vgt, ap, dtile, nc=nc, nlev=nlev)
    return yt.swapaxes(0, 1).reshape(T, width)


@jax.jit
def _forward(x, w_in, w_out, norm_mix, norm_mlp, w_up, w_down, norm_final, rel_bias,
             lambda_q1, lambda_k1, lambda_q2, lambda_k2, subln_gain,
             ssm_lambda_re, ssm_lambda_im, ssm_log_dt, ssm_b_re, ssm_b_im,
             ssm_c_re, ssm_c_im, ssm_d, w_glu, b_glu):
    batch, seq, d_model = x.shape
    depth = w_in.shape[0]
    ssm_width = w_glu.shape[-1]
    attn_width = (w_in.shape[-1] - ssm_width) // 3
    T = batch * seq

    t_attn = _tile(seq, 512)
    bias = bias_tiles(rel_bias, t_attn, seq)

    w_out_b, w_glu_b, w_in_b = w_out.astype(BF16), w_glu.astype(BF16), w_in[0].astype(BF16)
    h = x.reshape(T, d_model).astype(F32)
    for i in range(depth):
        lam_init = 0.8 - 0.6 * math.exp(-0.3 * i)
        xn = rmsnorm(h, norm_mix[i], BF16)
        proj, w_up_b = matmul(xn, w_in_b, out_dtype=BF16, cast=(w_up, i), name="proj_in")
        lamv = jnp.stack([lambda_q1[i], lambda_k1[i], lambda_q2[i], lambda_k2[i]]).astype(F32)
        attn = diff_attention(proj, bias, rel_bias, lamv, subln_gain[i],
                              batch=batch, seq=seq, lam_init=lam_init, t=t_attn)
        y = s5_layer(proj[:, 3 * attn_width:], ssm_lambda_re[i], ssm_lambda_im[i], ssm_log_dt[i],
                     ssm_b_re[i], ssm_b_im[i], ssm_c_re[i], ssm_c_im[i], ssm_d[i], batch=batch, seq=seq)
        if i + 1 < depth:
            ssm, w_in_b = glu(y, w_glu_b, i, b_glu[i], cast=(w_in, i + 1))
        else:
            ssm = glu(y, w_glu_b, i, b_glu[i])
        h, hg, ssq = outproj(attn, ssm, w_out_b, i, h, norm_mlp[i])
        ff, w_down_b = matmul(hg, w_up_b, out_dtype=BF16, epilogue="relu2", cast=(w_down, i), row_ssq=ssq,
                              name="mlp_up")
        h = matmul(ff, w_down_b, out_dtype=F32, epilogue="residual", res=h, tk=4096, name="mlp_down")
    out = rmsnorm(h, norm_final, x.dtype)
    return out.reshape(batch, seq, d_model)


def kernel(x, w_in, w_out, norm_mix, norm_mlp, w_up, w_down, norm_final, rel_bias, lambda_q1, lambda_k1, lambda_q2, lambda_k2, subln_gain, ssm_lambda_re, ssm_lambda_im, ssm_log_dt, ssm_b_re, ssm_b_im, ssm_c_re, ssm_c_im, ssm_d, w_glu, b_glu):
    return _forward(x, w_in, w_out, norm_mix, norm_mlp, w_up, w_down, norm_final, rel_bias,
                    lambda_q1, lambda_k1, lambda_q2, lambda_k2, subln_gain,
                    ssm_lambda_re, ssm_lambda_im, ssm_log_dt, ssm_b_re, ssm_b_im,
                    ssm_c_re, ssm_c_im, ssm_d, w_glu, b_glu)
```

```python
import functools
import math

import numpy as np
import jax
import jax.numpy as jnp
from jax import lax
from jax.experimental import pallas as pl
from jax.experimental.pallas import tpu as pltpu

F32 = jnp.float32
BF16 = jnp.bfloat16

EPS = 1e-6
MAX_EXACT = 16
MAX_DISTANCE = 128
DIFF_HEAD_DIM = 64
SSM_GROUP = 16
SSM_CHUNK = 32
OCTET = 8
GROUPS_PER_ITER = 4
NEG_BIG = -1e30
LOG2E = math.log2(math.e)

V7X_VMEM_LIMIT_BYTES = 56 * 1024 * 1024
LANES = 128
BF16_SUBLANES = 16


def _cparams(*sem):
    return pltpu.CompilerParams(dimension_semantics=sem, vmem_limit_bytes=V7X_VMEM_LIMIT_BYTES)


def _tile(dim, pref):
    t = min(dim, pref)
    while dim % t:
        t //= 2
    return t


def _rmsnorm_kernel(x_ref, g_ref, o_ref):
    x = x_ref[...].astype(F32)
    ms = jnp.mean(x * x, axis=-1, keepdims=True)
    o_ref[...] = ((x * lax.rsqrt(ms + EPS)) * g_ref[...]).astype(o_ref.dtype)


def rmsnorm(x, g, out_dtype):
    m, d = x.shape
    tm = _tile(m, 256)
    return pl.pallas_call(
        _rmsnorm_kernel,
        grid=(m // tm,),
        in_specs=[pl.BlockSpec((tm, d), lambda i: (i, 0)),
                  pl.BlockSpec((1, d), lambda i: (0, 0))],
        out_specs=pl.BlockSpec((tm, d), lambda i: (i, 0)),
        out_shape=jax.ShapeDtypeStruct((m, d), out_dtype),
        compiler_params=_cparams("parallel"),
        name="rmsnorm",
    )(x, g.reshape(1, d).astype(F32))


def _mm_kernel(a_ref, b_ref, *rest, nk, epilogue, cast, mean_dim):
    rest = list(rest)
    r_ref = rest.pop(0) if epilogue == "residual" else None
    ssq_ref = rest.pop(0) if mean_dim else None
    if cast:
        c_ref, o_ref, co_ref = rest
        co_ref[...] = c_ref[...].astype(co_ref.dtype)
    else:
        (o_ref,) = rest

    if nk == 1:
        acc = jnp.dot(a_ref[...], b_ref[...], preferred_element_type=F32)
        if mean_dim:
            ms = jnp.sum(ssq_ref[...], axis=-1, keepdims=True) * (1.0 / mean_dim)
            acc = acc * lax.rsqrt(ms + EPS)
        if epilogue == "relu2":
            acc = jnp.square(jnp.maximum(acc, 0.0))
        elif epilogue == "residual":
            acc = r_ref[...] + acc
        o_ref[...] = acc.astype(o_ref.dtype)
        return

    @pl.when(pl.program_id(2) == 0)
    def _():
        o_ref[...] = r_ref[...]

    o_ref[...] += jnp.dot(a_ref[...], b_ref[...], preferred_element_type=F32)


def matmul(a, w, *, out_dtype, epilogue="none", res=None, cast=None, row_ssq=None,
           tm=1024, tn=1024, tk=4096, name="matmul"):
    m, kdim = a.shape
    _, n = w.shape
    tm, tn, tk = _tile(m, tm), _tile(n, tn), _tile(kdim, tk)
    gm, gn, nk = m // tm, n // tn, kdim // tk
    in_specs = [pl.BlockSpec((tm, tk), lambda i, j, k: (i, k)),
                pl.BlockSpec((tk, tn), lambda i, j, k: (k, j))]
    args = [a, w]
    if epilogue == "residual":
        in_specs.append(pl.BlockSpec((tm, tn), lambda i, j, k: (i, j)))
        args.append(res)
    if row_ssq is not None:
        assert nk == 1
        in_specs.append(pl.BlockSpec((tm, LANES), lambda i, j, k: (i, 0)))
        args.append(row_ssq)
    out_specs = [pl.BlockSpec((tm, tn), lambda i, j, k: (i, j))]
    out_shape = [jax.ShapeDtypeStruct((m, n), out_dtype)]
    if cast is not None:
        w_src, layer = cast
        _, ck, cn = w_src.shape
        rows = ck // (gm * gn * nk)
        assert rows * gm * gn * nk == ck and rows % BF16_SUBLANES == 0
        step = lambda i, j, k: (i * gn + j) * nk + k
        in_specs.append(pl.BlockSpec((None, rows, cn), lambda i, j, k: (layer, step(i, j, k), 0)))
        args.append(w_src)
        out_specs.append(pl.BlockSpec((rows, cn), lambda i, j, k: (step(i, j, k), 0)))
        out_shape.append(jax.ShapeDtypeStruct((ck, cn), a.dtype))
    assert nk == 1 or (epilogue == "residual" and out_dtype == F32)
    outs = pl.pallas_call(
        functools.partial(_mm_kernel, nk=nk, epilogue=epilogue, cast=cast is not None,
                          mean_dim=kdim if row_ssq is not None else 0),
        grid=(gm, gn, nk),
        in_specs=in_specs,
        out_specs=out_specs,
        out_shape=out_shape,
        compiler_params=_cparams("parallel", "parallel", "arbitrary"),
        name=name,
    )(*args)
    return outs if cast is not None else outs[0]


def _outproj_kernel(a1_ref, a2_ref, b1_ref, b2_ref, r_ref, g_ref, o_ref, hg_ref, ssq_ref):
    acc = jnp.dot(a1_ref[...], b1_ref[...], preferred_element_type=F32)
    acc = acc + jnp.dot(a2_ref[...], b2_ref[...], preferred_element_type=F32)
    h = r_ref[...] + acc
    o_ref[...] = h
    hg_ref[...] = (h * g_ref[...]).astype(hg_ref.dtype)
    h2 = h * h
    part = h2[:, 0:LANES]
    for c in range(1, h2.shape[1] // LANES):
        part = part + h2[:, c * LANES:(c + 1) * LANES]

    @pl.when(pl.program_id(1) == 0)
    def _():
        ssq_ref[...] = part

    @pl.when(pl.program_id(1) > 0)
    def _():
        ssq_ref[...] += part


def outproj(a1, a2, w, res, gain, *, tm=1024, tn=512):
    m, k1 = a1.shape
    _, k2 = a2.shape
    _, n = w.shape
    assert k1 == k2
    tm, tn = _tile(m, tm), _tile(n, tn)
    return pl.pallas_call(
        _outproj_kernel,
        grid=(m // tm, n // tn),
        in_specs=[pl.BlockSpec((tm, k1), lambda i, j: (i, 0)),
                  pl.BlockSpec((tm, k2), lambda i, j: (i, 0)),
                  pl.BlockSpec((k1, tn), lambda i, j: (0, j)),
                  pl.BlockSpec((k2, tn), lambda i, j: (1, j)),
                  pl.BlockSpec((tm, tn), lambda i, j: (i, j)),
                  pl.BlockSpec((1, tn), lambda i, j: (0, j))],
        out_specs=[pl.BlockSpec((tm, tn), lambda i, j: (i, j)),
                   pl.BlockSpec((tm, tn), lambda i, j: (i, j)),
                   pl.BlockSpec((tm, LANES), lambda i, j: (i, 0))],
        out_shape=[jax.ShapeDtypeStruct((m, n), F32),
                   jax.ShapeDtypeStruct((m, n), a1.dtype),
                   jax.ShapeDtypeStruct((m, LANES), F32)],
        compiler_params=_cparams("parallel", "arbitrary"),
        name="outproj",
    )(a1, a2, w, w, res, gain.reshape(1, n).astype(F32))


def _glu_kernel(y_ref, w_ref, b_ref, *rest, cast):
    if cast:
        c_ref, o_ref, co_ref = rest
        co_ref[...] = c_ref[...].astype(co_ref.dtype)
    else:
        (o_ref,) = rest
    y = y_ref[...]
    z = jnp.dot(y, w_ref[...], preferred_element_type=F32) + b_ref[...]
    o_ref[...] = (y.astype(F32) * jax.nn.sigmoid(z)).astype(o_ref.dtype)


def glu(y, w, layer, b, *, cast=None, tm=512):
    m, n = y.shape
    tm = _tile(m, tm)
    steps = m // tm
    in_specs = [pl.BlockSpec((tm, n), lambda i: (i, 0)),
                pl.BlockSpec((None, n, n), lambda i: (layer, 0, 0)),
                pl.BlockSpec((1, n), lambda i: (0, 0))]
    args = [y, w, b.reshape(1, n).astype(F32)]
    out_specs = [pl.BlockSpec((tm, n), lambda i: (i, 0))]
    out_shape = [jax.ShapeDtypeStruct((m, n), BF16)]
    if cast is not None:
        w_src, src_layer = cast
        _, ck, cn = w_src.shape
        rows = ck // steps
        assert rows * steps == ck and rows % BF16_SUBLANES == 0
        in_specs.append(pl.BlockSpec((None, rows, cn), lambda i: (src_layer, i, 0)))
        args.append(w_src)
        out_specs.append(pl.BlockSpec((rows, cn), lambda i: (i, 0)))
        out_shape.append(jax.ShapeDtypeStruct((ck, cn), y.dtype))
    outs = pl.pallas_call(
        functools.partial(_glu_kernel, cast=cast is not None),
        grid=(steps,),
        in_specs=in_specs,
        out_specs=out_specs,
        out_shape=out_shape,
        compiler_params=_cparams("parallel"),
        name="glu",
    )(*args)
    return outs if cast is not None else outs[0]


def _t5_bucket_np(n, num_buckets):
    n = np.maximum(n, 0)
    nf = np.maximum(n, 1).astype(np.float32)
    large = MAX_EXACT + (np.log(nf / np.float32(MAX_EXACT)) / np.float32(math.log(MAX_DISTANCE / MAX_EXACT))
                         * np.float32(num_buckets - MAX_EXACT)).astype(np.int32)
    large = np.minimum(large, num_buckets - 1)
    return np.where(n < MAX_EXACT, n, large).astype(np.int32)


def _bias_tiles_kernel(rb_ref, idx_ref, o_ref, *, num_buckets):
    h = pl.program_id(0)
    for d in range(idx_ref.shape[0]):
        idx = idx_ref[d]
        acc = jnp.full(idx.shape, NEG_BIG, F32)
        for k in range(num_buckets):
            acc = jnp.where(idx == k, rb_ref[k, h] * LOG2E, acc)
        o_ref[0, d] = acc


def bias_tiles(rel_bias, t, seq):
    num_buckets, n_heads = rel_bias.shape
    c = np.arange(t)[:, None]
    r = np.arange(t)[None, :]
    far = _t5_bucket_np(np.arange(t + 1, max(seq, t + 2)), num_buckets)
    assert np.all(far == far[0]), "kv tiles two or more steps left of the diagonal must share one bucket"
    idx_far = np.full((t, t), far[0])
    idx_left = _t5_bucket_np(t + r - c, num_buckets)
    idx_diag = np.where(r - c >= 0, _t5_bucket_np(r - c, num_buckets), -1)
    idx = jnp.asarray(np.stack([idx_far, idx_left, idx_diag]).astype(np.int32))
    return pl.pallas_call(
        functools.partial(_bias_tiles_kernel, num_buckets=num_buckets),
        grid=(n_heads,),
        in_specs=[pl.BlockSpec(memory_space=pltpu.SMEM),
                  pl.BlockSpec((3, t, t), lambda h: (0, 0, 0))],
        out_specs=pl.BlockSpec((1, 3, t, t), lambda h: (h, 0, 0, 0)),
        out_shape=jax.ShapeDtypeStruct((n_heads, 3, t, t), F32),
        compiler_params=_cparams("arbitrary"),
        name="bias_tiles",
    )(rel_bias.astype(F32), idx)


def _attn_kernel(lamv_ref, gain_ref, bias_ref, q_ref, k_ref, v_ref, o_ref,
                 vt_ref, qt_ref, sa_ref, sb_ref, mxa_ref, mxb_ref, m_ref, l_ref, acc_ref, *, t, lam_init):
    qi = pl.program_id(2)
    d = DIFF_HEAD_DIM
    hd = 2 * d
    nh, nkv = vt_ref.shape[0], vt_ref.shape[1]
    heads = range(nh)

    def lanes(e):
        return slice(e * hd, (e + 1) * hd)

    @pl.when(qi == 0)
    def _():
        for e in heads:
            for c in range(nkv):
                vt_ref[e, c] = v_ref[c * t:(c + 1) * t, lanes(e)].astype(F32).T.astype(vt_ref.dtype)

    row = lax.broadcasted_iota(jnp.int32, (hd, t), 0)
    for e in heads:
        qT = (q_ref[:, lanes(e)].astype(F32) * (d ** -0.5 * LOG2E)).T
        qt_ref[e, :, 0:t] = jnp.where(row < d, qT, 0.0).astype(qt_ref.dtype)
        qt_ref[e, :, t:2 * t] = jnp.where(row >= d, qT, 0.0).astype(qt_ref.dtype)

    m_ref[...] = jnp.full(m_ref.shape, NEG_BIG, F32)
    l_ref[...] = jnp.zeros(l_ref.shape, F32)
    acc_ref[...] = jnp.zeros(acc_ref.shape, F32)

    buf_a = (sa_ref, mxa_ref)
    buf_b = (sb_ref, mxb_ref)

    def scores(e, j, buf):
        s_ref, mx_ref = buf
        kblk = k_ref[pl.ds(pl.multiple_of(j * t, t), t), lanes(e)]
        s = jnp.dot(kblk, qt_ref[e], preferred_element_type=F32)
        b = bias_ref[e, jnp.clip(j - qi + 2, 0, 2)]
        for c in range(2):
            sc = s[:, c * t:(c + 1) * t] + b
            s_ref[e, :, c * t:(c + 1) * t] = sc
            mx_ref[e, :, c * t:(c + 1) * t] = jnp.max(sc, axis=0, keepdims=True)

    def reduce(e, buf):
        s_ref, mx_ref = buf
        m_prev = m_ref[e]
        m_new = jnp.maximum(m_prev, mx_ref[e])
        alpha = jnp.exp2(m_prev - m_new)
        p = jnp.exp2(s_ref[e] - m_new)
        l_ref[e] = alpha * l_ref[e] + jnp.sum(p, axis=0, keepdims=True)
        m_ref[e] = m_new
        return alpha, p.astype(vt_ref.dtype)

    def accumulate(e, j, alpha, p):
        acc_ref[e] = alpha * acc_ref[e] + jnp.dot(vt_ref[e, j], p, preferred_element_type=F32)

    def step(j, cur, nxt):
        ap = [reduce(e, cur) for e in heads]
        for e in heads:
            scores(e, j + 1, nxt)
        for e in heads:
            accumulate(e, j, *ap[e])

    odd = (qi & 1) == 1

    @pl.when(jnp.logical_not(odd))
    def _():
        for e in heads:
            scores(e, 0, buf_a)

    @pl.when(odd)
    def _():
        for e in heads:
            scores(e, 0, buf_b)
        step(0, buf_b, buf_a)

    def pair(pp, carry):
        j = (qi & 1) + 2 * pp
        step(j, buf_a, buf_b)
        step(j + 1, buf_b, buf_a)
        return carry

    lax.fori_loop(0, lax.shift_right_logical(qi, 1), pair, 0)
    ap = [reduce(e, buf_a) for e in heads]
    for e in heads:
        accumulate(e, qi, *ap[e])

    lv = lamv_ref[...]
    lam = (jnp.exp(jnp.sum(lv[0:1] * lv[1:2], axis=-1, keepdims=True))
           - jnp.exp(jnp.sum(lv[2:3] * lv[3:4], axis=-1, keepdims=True)) + lam_init)
    for e in heads:
        on = acc_ref[e] * (1.0 / l_ref[e])
        o = on[:, 0:t] - lam * on[:, t:2 * t]
        o = o * lax.rsqrt(jnp.mean(o * o, axis=0, keepdims=True) + EPS)
        o = (o * gain_ref[...]) * (1.0 - lam_init)
        o_ref[:, lanes(e)] = o.T.astype(o_ref.dtype)


def diff_attention(qkv, bias, rel_bias, lamv, gain, *, batch, seq, lam_init, t):
    n_heads = rel_bias.shape[1]
    hd = 2 * DIFF_HEAD_DIM
    nq = seq // t
    nh = 2 if n_heads % 2 == 0 else 1
    ng = n_heads // nh
    kernel = functools.partial(_attn_kernel, t=t, lam_init=lam_init)
    stat = pltpu.VMEM((nh, 1, 2 * t), F32)
    tile = pltpu.VMEM((nh, t, 2 * t), F32)
    return pl.pallas_call(
        kernel,
        grid=(batch, ng, nq),
        in_specs=[pl.BlockSpec((4, DIFF_HEAD_DIM), lambda b, h, i: (0, 0)),
                  pl.BlockSpec((hd, 1), lambda b, h, i: (0, 0)),
                  pl.BlockSpec((nh, 3, t, t), lambda b, h, i: (h, 0, 0, 0)),
                  pl.BlockSpec((t, nh * hd), lambda b, h, i: (b * nq + i, h)),
                  pl.BlockSpec((seq, nh * hd), lambda b, h, i: (b, ng + h)),
                  pl.BlockSpec((seq, nh * hd), lambda b, h, i: (b, 2 * ng + h))],
        out_specs=pl.BlockSpec((t, nh * hd), lambda b, h, i: (b * nq + i, h)),
        out_shape=jax.ShapeDtypeStruct((batch * seq, n_heads * hd), BF16),
        scratch_shapes=[pltpu.VMEM((nh, nq, hd, t), BF16), pltpu.VMEM((nh, hd, 2 * t), BF16),
                        tile, tile, stat, stat, stat, stat,
                        pltpu.VMEM((nh, hd, 2 * t), F32)],
        compiler_params=_cparams("parallel", "parallel", "arbitrary"),
        name="diff_attention",
    )(lamv, gain.reshape(hd, 1).astype(F32), bias, qkv, qkv, qkv)


def _s5_prep_kernel(lre_ref, lim_ref, ldt_ref, btr_ref, bti_ref, cr_ref, ci_ref,
                    tg_ref, wg_ref, vgt_ref, ap_ref, es_ref, qs_ref, *, gb, chunk, nlev):
    L = chunk
    P = lre_ref.shape[-1]
    W = SSM_GROUP * L
    rows = es_ref.shape[2]
    lane = lax.broadcasted_iota(jnp.int32, (1, 2 * P), 1)
    sgn = jnp.where(lane < P, -1.0, 1.0).astype(F32)
    jj = lax.broadcasted_iota(jnp.int32, (rows, 2 * P), 0).astype(F32)

    def dup(x):
        return jnp.concatenate([x, x], axis=-1)

    def group(g, e_ref, q_ref):
        dt = jnp.exp(ldt_ref[g])
        lr, li = lre_ref[g], lim_ref[g]
        lrdt, lidt = lr * dt, li * dt
        mag = jnp.exp(lrdt)
        ar, ai = mag * jnp.cos(lidt), mag * jnp.sin(lidt)
        den = lr * lr + li * li
        nr, ni = ar - 1.0, ai
        gr = (nr * lr + ni * li) / den
        gi = (ni * lr - nr * li) / den
        btr, bti = btr_ref[g], bti_ref[g]
        bbr = gr * btr - gi * bti
        bbi = gr * bti + gi * btr
        cr, ci = cr_ref[g], ci_ref[g]
        bc1 = jnp.concatenate([bbr, bbi], axis=-1)
        bc2 = jnp.concatenate([-bbi, bbr], axis=-1)
        cc1 = jnp.concatenate([cr, -ci], axis=-1)
        cc2 = jnp.concatenate([-ci, -cr], axis=-1)

        mg = jnp.exp(jj * dup(lrdt))
        ang = jj * dup(lidt)
        e_ref[0] = mg * jnp.cos(ang)
        e_ref[1] = mg * jnp.sin(ang)

        for j in range(L + 1):
            er = jnp.broadcast_to(e_ref[0, j:j + 1, :], (SSM_GROUP, 2 * P))
            ei = jnp.broadcast_to(e_ref[1, j:j + 1, :], (SSM_GROUP, 2 * P))
            q_ref[j * SSM_GROUP:(j + 1) * SSM_GROUP, :] = er * cc1 + ei * cc2
            if j < L:
                tau = L - 1 - j
                wg_ref[g, tau * SSM_GROUP:(tau + 1) * SSM_GROUP, :] = (er * bc1 + ei * bc2).astype(wg_ref.dtype)

        vgt_ref[g] = q_ref[SSM_GROUP:SSM_GROUP * (L + 1), :].astype(vgt_ref.dtype)
        strip = lax.dot_general(bc1, q_ref[0:W, :], (((1,), (1,)), ((), ())),
                                precision=lax.Precision.HIGHEST, preferred_element_type=F32)
        pad = jnp.concatenate([jnp.zeros_like(strip), strip], axis=-1)
        lanes_per_step = LANES // SSM_GROUP
        for r in range(lanes_per_step):
            rolled = pad if r == 0 else pltpu.roll(pad, SSM_GROUP * r, axis=1)
            for qd in range(L // lanes_per_step):
                tau = lanes_per_step * qd + r
                tg_ref[g, tau * SSM_GROUP:(tau + 1) * SSM_GROUP, :] = (
                    rolled[:, W - LANES * qd:2 * W - LANES * qd].astype(tg_ref.dtype))

        pr, pi = e_ref[0, L:L + 1, :], e_ref[1, L:L + 1, :]
        for k in range(nlev):
            ap_ref[g, k, 0:1, :] = pr
            ap_ref[g, k, 1:2, :] = pi * sgn
            pr, pi = pr * pr - pi * pi, 2.0 * pr * pi

    def groups(gg, carry):
        for u in range(GROUPS_PER_ITER):
            group(gg * GROUPS_PER_ITER + u, es_ref.at[u], qs_ref.at[u])
        return carry

    lax.fori_loop(0, gb // GROUPS_PER_ITER, groups, 0)


def s5_prep(lam_re, lam_im, log_dt, bt_re, bt_im, c_re, c_im, *, chunk, nlev, gb=8):
    G, P = lam_re.shape
    H = SSM_GROUP
    W = H * chunk
    assert 2 * P == LANES and chunk % (LANES // H) == 0
    gb = _tile(G, gb)
    rows = -(-(chunk + 1) // 8) * 8
    vec = lambda a: a.reshape(G, 1, -1).astype(F32)
    g3 = lambda s1, s2: pl.BlockSpec((gb, s1, s2), lambda i: (i, 0, 0))
    return pl.pallas_call(
        functools.partial(_s5_prep_kernel, gb=gb, chunk=chunk, nlev=nlev),
        grid=(G // gb,),
        in_specs=[g3(1, P), g3(1, P), g3(1, 1), g3(H, P), g3(H, P), g3(H, P), g3(H, P)],
        out_specs=[g3(W, W), g3(W, 2 * P), g3(W, 2 * P),
                   pl.BlockSpec((gb, nlev, 2, 2 * P), lambda i: (i, 0, 0, 0))],
        out_shape=[jax.ShapeDtypeStruct((G, W, W), BF16),
                   jax.ShapeDtypeStruct((G, W, 2 * P), BF16),
                   jax.ShapeDtypeStruct((G, W, 2 * P), BF16),
                   jax.ShapeDtypeStruct((G, nlev, 2, 2 * P), F32)],
        scratch_shapes=[pltpu.VMEM((GROUPS_PER_ITER, 2, rows, 2 * P), F32),
                        pltpu.VMEM((GROUPS_PER_ITER, H * (chunk + 1), 2 * P), F32)],
        compiler_params=_cparams("parallel"),
        name="s5_prep",
    )(vec(lam_re), vec(lam_im), vec(log_dt), bt_re.astype(F32), bt_im.astype(F32),
      c_re.astype(F32), c_im.astype(F32))


def _s5_kernel(*refs, gb, nc, nlev, na, cast):
    refs = list(refs)
    x_refs, refs = refs[:na * OCTET], refs[na * OCTET:]
    perm_ref, tg_ref, wg_ref, vgt_ref, ap_ref, d_ref = refs[:6]
    if cast:
        c_ref, o_ref, co_ref, xs_ref, ys_ref = refs[6:]
        co_ref[...] = c_ref[...].astype(co_ref.dtype)
    else:
        o_ref, xs_ref, ys_ref = refs[6:]
    n = o_ref.shape[1]
    P2 = wg_ref.shape[-1]
    cidx = lax.broadcasted_iota(jnp.int32, (n, P2), 0) % nc

    def shift_rows(s, sh):
        return jnp.where(cidx >= sh, pltpu.roll(s, sh, axis=0), 0.0)

    for a in range(na):
        seg = jnp.concatenate([x_refs[OCTET * a + b][...] for b in range(OCTET)], axis=-1)
        seg = jnp.dot(seg, perm_ref[...], preferred_element_type=F32).astype(xs_ref.dtype)
        for g in range(gb):
            xs_ref[g, a] = seg[:, g * LANES:(g + 1) * LANES]

    def groups(gg, carry):
        gs = [gg * GROUPS_PER_ITER + u for u in range(GROUPS_PER_ITER)]
        xb = [jnp.concatenate([xs_ref[g, a] for a in range(na)], axis=-1) for g in gs]
        s = [jnp.dot(x, wg_ref[g], preferred_element_type=F32) for x, g in zip(xb, gs)]
        for k in range(nlev):
            sh = [shift_rows(v, 1 << k) for v in s]
            s = [v + w * ap_ref[g, k, 0:1, :] + pltpu.roll(w, P2 // 2, axis=1) * ap_ref[g, k, 1:2, :]
                 for v, w, g in zip(s, sh, gs)]
        for x, v, g in zip(xb, s, gs):
            s_in = shift_rows(v, 1)
            y = jnp.dot(x, tg_ref[g], preferred_element_type=F32)
            y = y + lax.dot_general(s_in.astype(BF16), vgt_ref[g], (((1,), (1,)), ((), ())),
                                    preferred_element_type=F32)
            y = y + x.astype(F32) * d_ref[g]
            yb = jax.nn.gelu(y).astype(ys_ref.dtype)
            for a in range(na):
                ys_ref[a, g] = yb[:, a * LANES:(a + 1) * LANES]
        return carry

    lax.fori_loop(0, gb // GROUPS_PER_ITER, groups, 0)

    for a in range(na):
        seg = jnp.concatenate([ys_ref[a, g] for g in range(gb)], axis=-1)
        seg = lax.dot_general(seg, perm_ref[...], (((1,), (1,)), ((), ())), preferred_element_type=F32)
        for b in range(OCTET):
            o_ref[OCTET * a + b] = seg[:, b * LANES:(b + 1) * LANES].astype(o_ref.dtype)


def s5_apply(proj2, L, col0, width, tg, wg, vgt, ap, dt, *, nc, nlev, cast=None):
    n, C = proj2.shape[0], proj2.shape[1] // L
    G = width // SSM_GROUP
    W = SSM_GROUP * L
    gb = LANES // SSM_GROUP
    assert G % gb == 0 and L % OCTET == 0 and col0 % LANES == 0 and C % LANES == 0
    steps = G // gb
    lane = np.arange(OCTET * LANES)
    b, g, h = lane // LANES, (lane % LANES) // SSM_GROUP, lane % SSM_GROUP
    perm = np.zeros((OCTET * LANES, OCTET * LANES), np.float32)
    perm[lane, g * (OCTET * SSM_GROUP) + b * SSM_GROUP + h] = 1.0
    g3 = lambda s1, s2: pl.BlockSpec((gb, s1, s2), lambda i: (i, 0, 0))
    slabs = [pl.BlockSpec((n, LANES), lambda i, t=t: (0, (t * C + col0) // LANES + i)) for t in range(L)]
    in_specs = slabs + [pl.BlockSpec(perm.shape, lambda i: (0, 0)), g3(W, W), g3(W, LANES), g3(W, LANES),
                        pl.BlockSpec((gb, nlev, 2, LANES), lambda i: (i, 0, 0, 0)), g3(1, W)]
    args = [proj2] * L + [jnp.asarray(perm, BF16), tg, wg, vgt, ap, dt]
    out_specs = [pl.BlockSpec((L, n, LANES), lambda i: (0, 0, i))]
    out_shape = [jax.ShapeDtypeStruct((L, n, width), BF16)]
    if cast is not None:
        w_src, layer = cast
        _, ck, cn = w_src.shape
        rows = ck // steps
        assert rows * steps == ck and rows % BF16_SUBLANES == 0
        in_specs.append(pl.BlockSpec((None, rows, cn), lambda i: (layer, i, 0)))
        args.append(w_src)
        out_specs.append(pl.BlockSpec((rows, cn), lambda i: (i, 0)))
        out_shape.append(jax.ShapeDtypeStruct((ck, cn), BF16))
    outs = pl.pallas_call(
        functools.partial(_s5_kernel, gb=gb, nc=nc, nlev=nlev, na=L // OCTET, cast=cast is not None),
        grid=(steps,),
        in_specs=in_specs,
        out_specs=out_specs,
        out_shape=out_shape,
        scratch_shapes=[pltpu.VMEM((gb, L // OCTET, n, LANES), BF16),
                        pltpu.VMEM((L // OCTET, gb, n, LANES), BF16)],
        compiler_params=_cparams("parallel"),
        name="s5_apply",
    )(*args)
    return outs if cast is not None else outs[0]


def s5_layer(proj, col0, width, lam_re, lam_im, log_dt, b_re, b_im, c_re, c_im, d, *, batch, seq, cast=None):
    T = proj.shape[0]
    G = width // SSM_GROUP
    L = _tile(seq, SSM_CHUNK)
    nc = seq // L
    nlev = max(nc - 1, 0).bit_length()
    n = batch * nc
    W = SSM_GROUP * L
    bt_re = jnp.swapaxes(b_re, -1, -2)
    bt_im = jnp.swapaxes(b_im, -1, -2)
    tg, wg, vgt, ap = s5_prep(lam_re, lam_im, log_dt, bt_re, bt_im, c_re, c_im, chunk=L, nlev=max(nlev, 1))
    dtile = jnp.tile(d.astype(F32), (1, L)).reshape(G, 1, W)
    outs = s5_apply(proj.reshape(n, -1), L, col0, width, tg, wg, vgt, ap, dtile, nc=nc, nlev=nlev, cast=cast)
    yt, rest = (outs[0], outs[1:]) if cast is not None else (outs, ())
    return (yt.swapaxes(0, 1).reshape(T, width), *rest)


@jax.jit
def _forward(x, w_in, w_out, norm_mix, norm_mlp, w_up, w_down, norm_final, rel_bias,
             lambda_q1, lambda_k1, lambda_q2, lambda_k2, subln_gain,
             ssm_lambda_re, ssm_lambda_im, ssm_log_dt, ssm_b_re, ssm_b_im,
             ssm_c_re, ssm_c_im, ssm_d, w_glu, b_glu):
    batch, seq, d_model = x.shape
    depth = w_in.shape[0]
    ssm_width = w_glu.shape[-1]
    attn_width = (w_in.shape[-1] - ssm_width) // 3
    T = batch * seq

    t_attn = _tile(seq, 512)
    bias = bias_tiles(rel_bias, t_attn, seq)

    w_glu_b, w_in_b = w_glu.astype(BF16), w_in[0].astype(BF16)
    h = x.reshape(T, d_model).astype(F32)
    for i in range(depth):
        lam_init = 0.8 - 0.6 * math.exp(-0.3 * i)
        xn = rmsnorm(h, norm_mix[i], BF16)
        proj, w_up_b = matmul(xn, w_in_b, out_dtype=BF16, cast=(w_up, i), name="proj_in")
        lamv = jnp.stack([lambda_q1[i], lambda_k1[i], lambda_q2[i], lambda_k2[i]]).astype(F32)
        attn = diff_attention(proj, bias, rel_bias, lamv, subln_gain[i],
                              batch=batch, seq=seq, lam_init=lam_init, t=t_attn)
        y, w_out_b = s5_layer(proj, 3 * attn_width, ssm_width, ssm_lambda_re[i], ssm_lambda_im[i],
                              ssm_log_dt[i], ssm_b_re[i], ssm_b_im[i], ssm_c_re[i], ssm_c_im[i], ssm_d[i],
                              batch=batch, seq=seq, cast=(w_out, i))
        if i + 1 < depth:
            ssm, w_in_b = glu(y, w_glu_b, i, b_glu[i], cast=(w_in, i + 1))
        else:
            ssm = glu(y, w_glu_b, i, b_glu[i])
        h, hg, ssq = outproj(attn, ssm, w_out_b, h, norm_mlp[i])
        ff, w_down_b = matmul(hg, w_up_b, out_dtype=BF16, epilogue="relu2", cast=(w_down, i), row_ssq=ssq,
                              name="mlp_up")
        h = matmul(ff, w_down_b, out_dtype=F32, epilogue="residual", res=h, tk=4096, name="mlp_down")
    out = rmsnorm(h, norm_final, x.dtype)
    return out.reshape(batch, seq, d_model)


def kernel(x, w_in, w_out, norm_mix, norm_mlp, w_up, w_down, norm_final, rel_bias, lambda_q1, lambda_k1, lambda_q2, lambda_k2, subln_gain, ssm_lambda_re, ssm_lambda_im, ssm_log_dt, ssm_b_re, ssm_b_im, ssm_c_re, ssm_c_im, ssm_d, w_glu, b_glu):
    return _forward(x, w_in, w_out, norm_mix, norm_mlp, w_up, w_down, norm_final, rel_bias,
                    lambda_q1, lambda_k1, lambda_q2, lambda_k2, subln_gain,
                    ssm_lambda_re, ssm_lambda_im, ssm_log_dt, ssm_b_re, ssm_b_im,
                    ssm_c_re, ssm_c_im, ssm_d, w_glu, b_glu)
```

```python
import functools
import math

import numpy as np
import jax
import jax.numpy as jnp
from jax import lax
from jax.experimental import pallas as pl
from jax.experimental.pallas import tpu as pltpu

F32 = jnp.float32
BF16 = jnp.bfloat16

EPS = 1e-6
MAX_EXACT = 16
MAX_DISTANCE = 128
DIFF_HEAD_DIM = 64
SSM_GROUP = 16
SSM_CHUNK = 32
OCTET = 8
GROUPS_PER_ITER = 8
NEG_BIG = -1e30
LOG2E = math.log2(math.e)

V7X_VMEM_LIMIT_BYTES = 56 * 1024 * 1024
LANES = 128
BF16_SUBLANES = 16


def _cparams(*sem):
    return pltpu.CompilerParams(dimension_semantics=sem, vmem_limit_bytes=V7X_VMEM_LIMIT_BYTES)


def _tile(dim, pref):
    t = min(dim, pref)
    while dim % t:
        t //= 2
    return t


def _rmsnorm_kernel(x_ref, g_ref, o_ref):
    x = x_ref[...].astype(F32)
    ms = jnp.mean(x * x, axis=-1, keepdims=True)
    o_ref[...] = ((x * lax.rsqrt(ms + EPS)) * g_ref[...]).astype(o_ref.dtype)


def rmsnorm(x, g, out_dtype):
    m, d = x.shape
    tm = _tile(m, 256)
    return pl.pallas_call(
        _rmsnorm_kernel,
        grid=(m // tm,),
        in_specs=[pl.BlockSpec((tm, d), lambda i: (i, 0)),
                  pl.BlockSpec((1, d), lambda i: (0, 0))],
        out_specs=pl.BlockSpec((tm, d), lambda i: (i, 0)),
        out_shape=jax.ShapeDtypeStruct((m, d), out_dtype),
        compiler_params=_cparams("parallel"),
        name="rmsnorm",
    )(x, g.reshape(1, d).astype(F32))


def _mm_kernel(a_ref, b_ref, *rest, nk, epilogue, cast, mean_dim):
    rest = list(rest)
    r_ref = rest.pop(0) if epilogue == "residual" else None
    ssq_ref = rest.pop(0) if mean_dim else None
    if cast:
        c_ref, o_ref, co_ref = rest
        co_ref[...] = c_ref[...].astype(co_ref.dtype)
    else:
        (o_ref,) = rest

    if nk == 1:
        acc = jnp.dot(a_ref[...], b_ref[...], preferred_element_type=F32)
        if mean_dim:
            ms = jnp.sum(ssq_ref[...], axis=-1, keepdims=True) * (1.0 / mean_dim)
            acc = acc * lax.rsqrt(ms + EPS)
        if epilogue == "relu2":
            acc = jnp.square(jnp.maximum(acc, 0.0))
        elif epilogue == "residual":
            acc = r_ref[...] + acc
        o_ref[...] = acc.astype(o_ref.dtype)
        return

    @pl.when(pl.program_id(2) == 0)
    def _():
        o_ref[...] = r_ref[...]

    o_ref[...] += jnp.dot(a_ref[...], b_ref[...], preferred_element_type=F32)


def matmul(a, w, *, out_dtype, epilogue="none", res=None, cast=None, row_ssq=None,
           tm=1024, tn=1024, tk=4096, name="matmul"):
    m, kdim = a.shape
    _, n = w.shape
    tm, tn, tk = _tile(m, tm), _tile(n, tn), _tile(kdim, tk)
    gm, gn, nk = m // tm, n // tn, kdim // tk
    in_specs = [pl.BlockSpec((tm, tk), lambda i, j, k: (i, k)),
                pl.BlockSpec((tk, tn), lambda i, j, k: (k, j))]
    args = [a, w]
    if epilogue == "residual":
        in_specs.append(pl.BlockSpec((tm, tn), lambda i, j, k: (i, j)))
        args.append(res)
    if row_ssq is not None:
        assert nk == 1
        in_specs.append(pl.BlockSpec((tm, LANES), lambda i, j, k: (i, 0)))
        args.append(row_ssq)
    out_specs = [pl.BlockSpec((tm, tn), lambda i, j, k: (i, j))]
    out_shape = [jax.ShapeDtypeStruct((m, n), out_dtype)]
    if cast is not None:
        w_src, layer = cast
        _, ck, cn = w_src.shape
        rows = ck // (gm * gn * nk)
        assert rows * gm * gn * nk == ck and rows % BF16_SUBLANES == 0
        step = lambda i, j, k: (i * gn + j) * nk + k
        in_specs.append(pl.BlockSpec((None, rows, cn), lambda i, j, k: (layer, step(i, j, k), 0)))
        args.append(w_src)
        out_specs.append(pl.BlockSpec((rows, cn), lambda i, j, k: (step(i, j, k), 0)))
        out_shape.append(jax.ShapeDtypeStruct((ck, cn), a.dtype))
    assert nk == 1 or (epilogue == "residual" and out_dtype == F32)
    outs = pl.pallas_call(
        functools.partial(_mm_kernel, nk=nk, epilogue=epilogue, cast=cast is not None,
                          mean_dim=kdim if row_ssq is not None else 0),
        grid=(gm, gn, nk),
        in_specs=in_specs,
        out_specs=out_specs,
        out_shape=out_shape,
        compiler_params=_cparams("parallel", "parallel", "arbitrary"),
        name=name,
    )(*args)
    return outs if cast is not None else outs[0]


def _outproj_kernel(a1_ref, a2_ref, b1_ref, b2_ref, r_ref, g_ref, o_ref, hg_ref, ssq_ref):
    acc = jnp.dot(a1_ref[...], b1_ref[...], preferred_element_type=F32)
    acc = acc + jnp.dot(a2_ref[...], b2_ref[...], preferred_element_type=F32)
    h = r_ref[...] + acc
    o_ref[...] = h
    hg_ref[...] = (h * g_ref[...]).astype(hg_ref.dtype)
    h2 = h * h
    part = h2[:, 0:LANES]
    for c in range(1, h2.shape[1] // LANES):
        part = part + h2[:, c * LANES:(c + 1) * LANES]

    @pl.when(pl.program_id(1) == 0)
    def _():
        ssq_ref[...] = part

    @pl.when(pl.program_id(1) > 0)
    def _():
        ssq_ref[...] += part


def outproj(a1, a2, w, layer, res, gain, *, tm=1024, tn=512):
    m, k1 = a1.shape
    _, k2 = a2.shape
    _, _, n = w.shape
    assert k1 == k2
    tm, tn = _tile(m, tm), _tile(n, tn)
    return pl.pallas_call(
        _outproj_kernel,
        grid=(m // tm, n // tn),
        in_specs=[pl.BlockSpec((tm, k1), lambda i, j: (i, 0)),
                  pl.BlockSpec((tm, k2), lambda i, j: (i, 0)),
                  pl.BlockSpec((None, k1, tn), lambda i, j: (layer, 0, j)),
                  pl.BlockSpec((None, k2, tn), lambda i, j: (layer, 1, j)),
                  pl.BlockSpec((tm, tn), lambda i, j: (i, j)),
                  pl.BlockSpec((1, tn), lambda i, j: (0, j))],
        out_specs=[pl.BlockSpec((tm, tn), lambda i, j: (i, j)),
                   pl.BlockSpec((tm, tn), lambda i, j: (i, j)),
                   pl.BlockSpec((tm, LANES), lambda i, j: (i, 0))],
        out_shape=[jax.ShapeDtypeStruct((m, n), F32),
                   jax.ShapeDtypeStruct((m, n), a1.dtype),
                   jax.ShapeDtypeStruct((m, LANES), F32)],
        compiler_params=_cparams("parallel", "arbitrary"),
        name="outproj",
    )(a1, a2, w, w, res, gain.reshape(1, n).astype(F32))


def _glu_kernel(y_ref, w_ref, b_ref, *rest, cast):
    if cast:
        c_ref, o_ref, co_ref = rest
        co_ref[...] = c_ref[...].astype(co_ref.dtype)
    else:
        (o_ref,) = rest
    y = y_ref[...]
    z = jnp.dot(y, w_ref[...], preferred_element_type=F32) + b_ref[...]
    o_ref[...] = (y.astype(F32) * jax.nn.sigmoid(z)).astype(o_ref.dtype)


def glu(y, w, layer, b, *, cast=None, tm=512):
    m, n = y.shape
    tm = _tile(m, tm)
    steps = m // tm
    in_specs = [pl.BlockSpec((tm, n), lambda i: (i, 0)),
                pl.BlockSpec((None, n, n), lambda i: (layer, 0, 0)),
                pl.BlockSpec((1, n), lambda i: (0, 0))]
    args = [y, w, b.reshape(1, n).astype(F32)]
    out_specs = [pl.BlockSpec((tm, n), lambda i: (i, 0))]
    out_shape = [jax.ShapeDtypeStruct((m, n), BF16)]
    if cast is not None:
        w_src, src_layer = cast
        _, ck, cn = w_src.shape
        rows = ck // steps
        assert rows * steps == ck and rows % BF16_SUBLANES == 0
        in_specs.append(pl.BlockSpec((None, rows, cn), lambda i: (src_layer, i, 0)))
        args.append(w_src)
        out_specs.append(pl.BlockSpec((rows, cn), lambda i: (i, 0)))
        out_shape.append(jax.ShapeDtypeStruct((ck, cn), y.dtype))
    outs = pl.pallas_call(
        functools.partial(_glu_kernel, cast=cast is not None),
        grid=(steps,),
        in_specs=in_specs,
        out_specs=out_specs,
        out_shape=out_shape,
        compiler_params=_cparams("parallel"),
        name="glu",
    )(*args)
    return outs if cast is not None else outs[0]


def _t5_bucket_np(n, num_buckets):
    n = np.maximum(n, 0)
    nf = np.maximum(n, 1).astype(np.float32)
    large = MAX_EXACT + (np.log(nf / np.float32(MAX_EXACT)) / np.float32(math.log(MAX_DISTANCE / MAX_EXACT))
                         * np.float32(num_buckets - MAX_EXACT)).astype(np.int32)
    large = np.minimum(large, num_buckets - 1)
    return np.where(n < MAX_EXACT, n, large).astype(np.int32)


def _bias_tiles_kernel(rb_ref, idx_ref, o_ref, *, num_buckets):
    h = pl.program_id(0)
    for d in range(idx_ref.shape[0]):
        idx = idx_ref[d]
        acc = jnp.full(idx.shape, NEG_BIG, F32)
        for k in range(num_buckets):
            acc = jnp.where(idx == k, rb_ref[k, h] * LOG2E, acc)
        o_ref[0, d] = acc


def bias_tiles(rel_bias, t, seq):
    num_buckets, n_heads = rel_bias.shape
    c = np.arange(t)[:, None]
    r = np.arange(t)[None, :]
    far = _t5_bucket_np(np.arange(t + 1, max(seq, t + 2)), num_buckets)
    assert np.all(far == far[0]), "kv tiles two or more steps left of the diagonal must share one bucket"
    idx_far = np.full((t, t), far[0])
    idx_left = _t5_bucket_np(t + r - c, num_buckets)
    idx_diag = np.where(r - c >= 0, _t5_bucket_np(r - c, num_buckets), -1)
    idx = jnp.asarray(np.stack([idx_far, idx_left, idx_diag]).astype(np.int32))
    return pl.pallas_call(
        functools.partial(_bias_tiles_kernel, num_buckets=num_buckets),
        grid=(n_heads,),
        in_specs=[pl.BlockSpec(memory_space=pltpu.SMEM),
                  pl.BlockSpec((3, t, t), lambda h: (0, 0, 0))],
        out_specs=pl.BlockSpec((1, 3, t, t), lambda h: (h, 0, 0, 0)),
        out_shape=jax.ShapeDtypeStruct((n_heads, 3, t, t), F32),
        compiler_params=_cparams("arbitrary"),
        name="bias_tiles",
    )(rel_bias.astype(F32), idx)


def _attn_kernel(lamv_ref, gain_ref, bias_ref, q_ref, k_ref, v_ref, o_ref,
                 vt_ref, qt_ref, sa_ref, sb_ref, mxa_ref, mxb_ref, m_ref, l_ref, acc_ref, *, t, lam_init):
    qi = pl.program_id(2)
    d = DIFF_HEAD_DIM
    hd = 2 * d
    nh, nkv = vt_ref.shape[0], vt_ref.shape[1]
    heads = range(nh)

    def lanes(e):
        return slice(e * hd, (e + 1) * hd)

    @pl.when(qi == 0)
    def _():
        for e in heads:
            for c in range(nkv):
                vt_ref[e, c] = v_ref[c * t:(c + 1) * t, lanes(e)].astype(F32).T.astype(vt_ref.dtype)

    row = lax.broadcasted_iota(jnp.int32, (hd, t), 0)
    for e in heads:
        qT = (q_ref[:, lanes(e)].astype(F32) * (d ** -0.5 * LOG2E)).T
        qt_ref[e, :, 0:t] = jnp.where(row < d, qT, 0.0).astype(qt_ref.dtype)
        qt_ref[e, :, t:2 * t] = jnp.where(row >= d, qT, 0.0).astype(qt_ref.dtype)

    m_ref[...] = jnp.full(m_ref.shape, NEG_BIG, F32)
    l_ref[...] = jnp.zeros(l_ref.shape, F32)
    acc_ref[...] = jnp.zeros(acc_ref.shape, F32)

    buf_a = (sa_ref, mxa_ref)
    buf_b = (sb_ref, mxb_ref)

    def scores(e, j, buf):
        s_ref, mx_ref = buf
        kblk = k_ref[pl.ds(pl.multiple_of(j * t, t), t), lanes(e)]
        s = jnp.dot(kblk, qt_ref[e], preferred_element_type=F32)
        b = bias_ref[e, jnp.clip(j - qi + 2, 0, 2)]
        for c in range(2):
            sc = s[:, c * t:(c + 1) * t] + b
            s_ref[e, :, c * t:(c + 1) * t] = sc
            mx_ref[e, :, c * t:(c + 1) * t] = jnp.max(sc, axis=0, keepdims=True)

    def reduce(e, buf):
        s_ref, mx_ref = buf
        m_prev = m_ref[e]
        m_new = jnp.maximum(m_prev, mx_ref[e])
        alpha = jnp.exp2(m_prev - m_new)
        p = jnp.exp2(s_ref[e] - m_new)
        l_ref[e] = alpha * l_ref[e] + jnp.sum(p, axis=0, keepdims=True)
        m_ref[e] = m_new
        return alpha, p.astype(vt_ref.dtype)

    def accumulate(e, j, alpha, p):
        acc_ref[e] = alpha * acc_ref[e] + jnp.dot(vt_ref[e, j], p, preferred_element_type=F32)

    def step(j, cur, nxt):
        ap = [reduce(e, cur) for e in heads]
        for e in heads:
            scores(e, j + 1, nxt)
        for e in heads:
            accumulate(e, j, *ap[e])

    odd = (qi & 1) == 1

    @pl.when(jnp.logical_not(odd))
    def _():
        for e in heads:
            scores(e, 0, buf_a)

    @pl.when(odd)
    def _():
        for e in heads:
            scores(e, 0, buf_b)
        step(0, buf_b, buf_a)

    def pair(pp, carry):
        j = (qi & 1) + 2 * pp
        step(j, buf_a, buf_b)
        step(j + 1, buf_b, buf_a)
        return carry

    lax.fori_loop(0, lax.shift_right_logical(qi, 1), pair, 0)
    ap = [reduce(e, buf_a) for e in heads]
    for e in heads:
        accumulate(e, qi, *ap[e])

    lv = lamv_ref[...]
    lam = (jnp.exp(jnp.sum(lv[0:1] * lv[1:2], axis=-1, keepdims=True))
           - jnp.exp(jnp.sum(lv[2:3] * lv[3:4], axis=-1, keepdims=True)) + lam_init)
    for e in heads:
        on = acc_ref[e] * (1.0 / l_ref[e])
        o = on[:, 0:t] - lam * on[:, t:2 * t]
        o = o * lax.rsqrt(jnp.mean(o * o, axis=0, keepdims=True) + EPS)
        o = (o * gain_ref[...]) * (1.0 - lam_init)
        o_ref[:, lanes(e)] = o.T.astype(o_ref.dtype)


def diff_attention(qkv, bias, rel_bias, lamv, gain, *, batch, seq, lam_init, t):
    n_heads = rel_bias.shape[1]
    hd = 2 * DIFF_HEAD_DIM
    nq = seq // t
    nh = 2 if n_heads % 2 == 0 else 1
    ng = n_heads // nh
    kernel = functools.partial(_attn_kernel, t=t, lam_init=lam_init)
    stat = pltpu.VMEM((nh, 1, 2 * t), F32)
    tile = pltpu.VMEM((nh, t, 2 * t), F32)
    return pl.pallas_call(
        kernel,
        grid=(batch, ng, nq),
        in_specs=[pl.BlockSpec((4, DIFF_HEAD_DIM), lambda b, h, i: (0, 0)),
                  pl.BlockSpec((hd, 1), lambda b, h, i: (0, 0)),
                  pl.BlockSpec((nh, 3, t, t), lambda b, h, i: (h, 0, 0, 0)),
                  pl.BlockSpec((t, nh * hd), lambda b, h, i: (b * nq + i, h)),
                  pl.BlockSpec((seq, nh * hd), lambda b, h, i: (b, ng + h)),
                  pl.BlockSpec((seq, nh * hd), lambda b, h, i: (b, 2 * ng + h))],
        out_specs=pl.BlockSpec((t, nh * hd), lambda b, h, i: (b * nq + i, h)),
        out_shape=jax.ShapeDtypeStruct((batch * seq, n_heads * hd), BF16),
        scratch_shapes=[pltpu.VMEM((nh, nq, hd, t), BF16), pltpu.VMEM((nh, hd, 2 * t), BF16),
                        tile, tile, stat, stat, stat, stat,
                        pltpu.VMEM((nh, hd, 2 * t), F32)],
        compiler_params=_cparams("parallel", "parallel", "arbitrary"),
        name="diff_attention",
    )(lamv, gain.reshape(hd, 1).astype(F32), bias, qkv, qkv, qkv)


def _s5_prep_kernel(lre_ref, lim_ref, ldt_ref, btr_ref, bti_ref, cr_ref, ci_ref,
                    tg_ref, wg_ref, vgt_ref, ap_ref, es_ref, qs_ref, *, gb, chunk, nlev):
    L = chunk
    P = lre_ref.shape[-1]
    W = SSM_GROUP * L
    rows = es_ref.shape[2]
    lane = lax.broadcasted_iota(jnp.int32, (1, 2 * P), 1)
    sgn = jnp.where(lane < P, -1.0, 1.0).astype(F32)
    jj = lax.broadcasted_iota(jnp.int32, (rows, 2 * P), 0).astype(F32)

    def dup(x):
        return jnp.concatenate([x, x], axis=-1)

    def group(g, e_ref, q_ref):
        dt = jnp.exp(ldt_ref[g])
        lr, li = lre_ref[g], lim_ref[g]
        lrdt, lidt = lr * dt, li * dt
        mag = jnp.exp(lrdt)
        ar, ai = mag * jnp.cos(lidt), mag * jnp.sin(lidt)
        den = lr * lr + li * li
        nr, ni = ar - 1.0, ai
        gr = (nr * lr + ni * li) / den
        gi = (ni * lr - nr * li) / den
        btr, bti = btr_ref[g], bti_ref[g]
        bbr = gr * btr - gi * bti
        bbi = gr * bti + gi * btr
        cr, ci = cr_ref[g], ci_ref[g]
        bc1 = jnp.concatenate([bbr, bbi], axis=-1)
        bc2 = jnp.concatenate([-bbi, bbr], axis=-1)
        cc1 = jnp.concatenate([cr, -ci], axis=-1)
        cc2 = jnp.concatenate([-ci, -cr], axis=-1)

        mg = jnp.exp(jj * dup(lrdt))
        ang = jj * dup(lidt)
        e_ref[0] = mg * jnp.cos(ang)
        e_ref[1] = mg * jnp.sin(ang)

        for j in range(L + 1):
            er = jnp.broadcast_to(e_ref[0, j:j + 1, :], (SSM_GROUP, 2 * P))
            ei = jnp.broadcast_to(e_ref[1, j:j + 1, :], (SSM_GROUP, 2 * P))
            q_ref[j * SSM_GROUP:(j + 1) * SSM_GROUP, :] = er * cc1 + ei * cc2
            if j < L:
                tau = L - 1 - j
                wg_ref[g, tau * SSM_GROUP:(tau + 1) * SSM_GROUP, :] = (er * bc1 + ei * bc2).astype(wg_ref.dtype)

        vgt_ref[g] = q_ref[SSM_GROUP:SSM_GROUP * (L + 1), :].astype(vgt_ref.dtype)
        strip = lax.dot_general(bc1, q_ref[0:W, :], (((1,), (1,)), ((), ())),
                                precision=lax.Precision.HIGHEST, preferred_element_type=F32)
        pad = jnp.concatenate([jnp.zeros_like(strip), strip], axis=-1)
        lanes_per_step = LANES // SSM_GROUP
        for r in range(lanes_per_step):
            rolled = pad if r == 0 else pltpu.roll(pad, SSM_GROUP * r, axis=1)
            for qd in range(L // lanes_per_step):
                tau = lanes_per_step * qd + r
                tg_ref[g, tau * SSM_GROUP:(tau + 1) * SSM_GROUP, :] = (
                    rolled[:, W - LANES * qd:2 * W - LANES * qd].astype(tg_ref.dtype))

        pr, pi = e_ref[0, L:L + 1, :], e_ref[1, L:L + 1, :]
        for k in range(nlev):
            ap_ref[g, k, 0:1, :] = pr
            ap_ref[g, k, 1:2, :] = pi * sgn
            pr, pi = pr * pr - pi * pi, 2.0 * pr * pi

    def groups(gg, carry):
        for u in range(GROUPS_PER_ITER):
            group(gg * GROUPS_PER_ITER + u, es_ref.at[u], qs_ref.at[u])
        return carry

    lax.fori_loop(0, gb // GROUPS_PER_ITER, groups, 0)


def s5_prep(lam_re, lam_im, log_dt, bt_re, bt_im, c_re, c_im, *, chunk, nlev, gb=8):
    G, P = lam_re.shape
    H = SSM_GROUP
    W = H * chunk
    assert 2 * P == LANES and chunk % (LANES // H) == 0
    gb = _tile(G, gb)
    rows = -(-(chunk + 1) // 8) * 8
    vec = lambda a: a.reshape(G, 1, -1).astype(F32)
    g3 = lambda s1, s2: pl.BlockSpec((gb, s1, s2), lambda i: (i, 0, 0))
    return pl.pallas_call(
        functools.partial(_s5_prep_kernel, gb=gb, chunk=chunk, nlev=nlev),
        grid=(G // gb,),
        in_specs=[g3(1, P), g3(1, P), g3(1, 1), g3(H, P), g3(H, P), g3(H, P), g3(H, P)],
        out_specs=[g3(W, W), g3(W, 2 * P), g3(W, 2 * P),
                   pl.BlockSpec((gb, nlev, 2, 2 * P), lambda i: (i, 0, 0, 0))],
        out_shape=[jax.ShapeDtypeStruct((G, W, W), BF16),
                   jax.ShapeDtypeStruct((G, W, 2 * P), BF16),
                   jax.ShapeDtypeStruct((G, W, 2 * P), BF16),
                   jax.ShapeDtypeStruct((G, nlev, 2, 2 * P), F32)],
        scratch_shapes=[pltpu.VMEM((GROUPS_PER_ITER, 2, rows, 2 * P), F32),
                        pltpu.VMEM((GROUPS_PER_ITER, H * (chunk + 1), 2 * P), F32)],
        compiler_params=_cparams("parallel"),
        name="s5_prep",
    )(vec(lam_re), vec(lam_im), vec(log_dt), bt_re.astype(F32), bt_im.astype(F32),
      c_re.astype(F32), c_im.astype(F32))


def _s5_kernel(x_ref, perm_ref, tg_ref, wg_ref, vgt_ref, ap_ref, d_ref, o_ref, xs_ref, ys_ref,
               *, gb, nc, nlev, na):
    n = x_ref.shape[1]
    P2 = wg_ref.shape[-1]
    cidx = lax.broadcasted_iota(jnp.int32, (n, P2), 0) % nc

    def shift_rows(s, sh):
        return jnp.where(cidx >= sh, pltpu.roll(s, sh, axis=0), 0.0)

    for a in range(na):
        seg = jnp.concatenate([x_ref[OCTET * a + b] for b in range(OCTET)], axis=-1)
        seg = jnp.dot(seg, perm_ref[...], preferred_element_type=F32).astype(xs_ref.dtype)
        for g in range(gb):
            xs_ref[g, a] = seg[:, g * LANES:(g + 1) * LANES]

    def groups(gg, carry):
        gs = [gg * GROUPS_PER_ITER + u for u in range(GROUPS_PER_ITER)]
        xb = [jnp.concatenate([xs_ref[g, a] for a in range(na)], axis=-1) for g in gs]
        s = [jnp.dot(x, wg_ref[g], preferred_element_type=F32) for x, g in zip(xb, gs)]
        for k in range(nlev):
            sh = [shift_rows(v, 1 << k) for v in s]
            s = [v + w * ap_ref[g, k, 0:1, :] + pltpu.roll(w, P2 // 2, axis=1) * ap_ref[g, k, 1:2, :]
                 for v, w, g in zip(s, sh, gs)]
        for x, v, g in zip(xb, s, gs):
            s_in = shift_rows(v, 1)
            y = jnp.dot(x, tg_ref[g], preferred_element_type=F32)
            y = y + lax.dot_general(s_in.astype(BF16), vgt_ref[g], (((1,), (1,)), ((), ())),
                                    preferred_element_type=F32)
            y = y + x.astype(F32) * d_ref[g]
            yb = jax.nn.gelu(y).astype(ys_ref.dtype)
            for a in range(na):
                ys_ref[a, g] = yb[:, a * LANES:(a + 1) * LANES]
        return carry

    lax.fori_loop(0, gb // GROUPS_PER_ITER, groups, 0)

    for a in range(na):
        seg = jnp.concatenate([ys_ref[a, g] for g in range(gb)], axis=-1)
        seg = lax.dot_general(seg, perm_ref[...], (((1,), (1,)), ((), ())), preferred_element_type=F32)
        for b in range(OCTET):
            o_ref[OCTET * a + b] = seg[:, b * LANES:(b + 1) * LANES].astype(o_ref.dtype)


def s5_apply(ut, tg, wg, vgt, ap, dt, *, nc, nlev):
    L, n, width = ut.shape
    G = width // SSM_GROUP
    W = SSM_GROUP * L
    gb = LANES // SSM_GROUP
    assert G % gb == 0 and L % OCTET == 0 and gb * SSM_GROUP * OCTET == OCTET * LANES
    lane = np.arange(OCTET * LANES)
    b, g, h = lane // LANES, (lane % LANES) // SSM_GROUP, lane % SSM_GROUP
    perm = np.zeros((OCTET * LANES, OCTET * LANES), np.float32)
    perm[lane, g * (OCTET * SSM_GROUP) + b * SSM_GROUP + h] = 1.0
    g3 = lambda s1, s2: pl.BlockSpec((gb, s1, s2), lambda i: (i, 0, 0))
    io = pl.BlockSpec((L, n, LANES), lambda i: (0, 0, i))
    return pl.pallas_call(
        functools.partial(_s5_kernel, gb=gb, nc=nc, nlev=nlev, na=L // OCTET),
        grid=(G // gb,),
        in_specs=[io, pl.BlockSpec(perm.shape, lambda i: (0, 0)), g3(W, W), g3(W, LANES), g3(W, LANES),
                  pl.BlockSpec((gb, nlev, 2, LANES), lambda i: (i, 0, 0, 0)), g3(1, W)],
        out_specs=io,
        out_shape=jax.ShapeDtypeStruct((L, n, width), BF16),
        scratch_shapes=[pltpu.VMEM((gb, L // OCTET, n, LANES), BF16),
                        pltpu.VMEM((L // OCTET, gb, n, LANES), BF16)],
        compiler_params=_cparams("parallel"),
        name="s5_apply",
    )(ut, jnp.asarray(perm, BF16), tg, wg, vgt, ap, dt)


def s5_layer(u, lam_re, lam_im, log_dt, b_re, b_im, c_re, c_im, d, *, batch, seq):
    T, width = u.shape
    G = width // SSM_GROUP
    L = _tile(seq, SSM_CHUNK)
    nc = seq // L
    nlev = max(nc - 1, 0).bit_length()
    n = batch * nc
    W = SSM_GROUP * L
    bt_re = jnp.swapaxes(b_re, -1, -2)
    bt_im = jnp.swapaxes(b_im, -1, -2)
    tg, wg, vgt, ap = s5_prep(lam_re, lam_im, log_dt, bt_re, bt_im, c_re, c_im, chunk=L, nlev=max(nlev, 1))
    ut = u.reshape(n, L, width).swapaxes(0, 1)
    dtile = jnp.tile(d.astype(F32), (1, L)).reshape(G, 1, W)
    yt = s5_apply(ut, tg, wg, vgt, ap, dtile, nc=nc, nlev=nlev)
    return yt.swapaxes(0, 1).reshape(T, width)


@jax.jit
def _forward(x, w_in, w_out, norm_mix, norm_mlp, w_up, w_down, norm_final, rel_bias,
             lambda_q1, lambda_k1, lambda_q2, lambda_k2, subln_gain,
             ssm_lambda_re, ssm_lambda_im, ssm_log_dt, ssm_b_re, ssm_b_im,
             ssm_c_re, ssm_c_im, ssm_d, w_glu, b_glu):
    batch, seq, d_model = x.shape
    depth = w_in.shape[0]
    ssm_width = w_glu.shape[-1]
    attn_width = (w_in.shape[-1] - ssm_width) // 3
    T = batch * seq

    t_attn = _tile(seq, 512)
    bias = bias_tiles(rel_bias, t_attn, seq)

    w_out_b, w_glu_b, w_in_b = w_out.astype(BF16), w_glu.astype(BF16), w_in[0].astype(BF16)
    h = x.reshape(T, d_model).astype(F32)
    for i in range(depth):
        lam_init = 0.8 - 0.6 * math.exp(-0.3 * i)
        xn = rmsnorm(h, norm_mix[i], BF16)
        proj, w_up_b = matmul(xn, w_in_b, out_dtype=BF16, cast=(w_up, i), name="proj_in")
        lamv = jnp.stack([lambda_q1[i], lambda_k1[i], lambda_q2[i], lambda_k2[i]]).astype(F32)
        attn = diff_attention(proj, bias, rel_bias, lamv, subln_gain[i],
                              batch=batch, seq=seq, lam_init=lam_init, t=t_attn)
        y = s5_layer(proj[:, 3 * attn_width:], ssm_lambda_re[i], ssm_lambda_im[i], ssm_log_dt[i],
                     ssm_b_re[i], ssm_b_im[i], ssm_c_re[i], ssm_c_im[i], ssm_d[i], batch=batch, seq=seq)
        if i + 1 < depth:
            ssm, w_in_b = glu(y, w_glu_b, i, b_glu[i], cast=(w_in, i + 1))
        else:
            ssm = glu(y, w_glu_b, i, b_glu[i])
        h, hg, ssq = outproj(attn, ssm, w_out_b, i, h, norm_mlp[i])
        ff, w_down_b = matmul(hg, w_up_b, out_dtype=BF16, epilogue="relu2", cast=(w_down, i), row_ssq=ssq,
                              name="mlp_up")
        h = matmul(ff, w_down_b, out_dtype=F32, epilogue="residual", res=h, tk=4096, name="mlp_down")
    out = rmsnorm(h, norm_final, x.dtype)
    return out.reshape(batch, seq, d_model)


def kernel(x, w_in, w_out, norm_mix, norm_mlp, w_up, w_down, norm_final, rel_bias, lambda_q1, lambda_k1, lambda_q2, lambda_k2, subln_gain, ssm_lambda_re, ssm_lambda_im, ssm_log_dt, ssm_b_re, ssm_b_im, ssm_c_re, ssm_c_im, ssm_d, w_glu, b_glu):
    return _forward(x, w_in, w_out, norm_mix, norm_mlp, w_up, w_down, norm_final, rel_bias,
                    lambda_q1, lambda_k1, lambda_q2, lambda_k2, subln_gain,
                    ssm_lambda_re, ssm_lambda_im, ssm_log_dt, ssm_b_re, ssm_b_im,
                    ssm_c_re, ssm_c_im, ssm_d, w_glu, b_glu)
```

```python
import functools
import math

import numpy as np
import jax
import jax.numpy as jnp
from jax import lax
from jax.experimental import pallas as pl
from jax.experimental.pallas import tpu as pltpu

F32 = jnp.float32
BF16 = jnp.bfloat16

EPS = 1e-6
MAX_EXACT = 16
MAX_DISTANCE = 128
DIFF_HEAD_DIM = 64
SSM_GROUP = 16
SSM_CHUNK = 32
OCTET = 8
GROUPS_PER_ITER = 4
NEG_BIG = -1e30
LOG2E = math.log2(math.e)

V7X_VMEM_LIMIT_BYTES = 56 * 1024 * 1024
LANES = 128
BF16_SUBLANES = 16


def _cparams(*sem):
    return pltpu.CompilerParams(dimension_semantics=sem, vmem_limit_bytes=V7X_VMEM_LIMIT_BYTES)


def _tile(dim, pref):
    t = min(dim, pref)
    while dim % t:
        t //= 2
    return t


def _rmsnorm_kernel(x_ref, g_ref, o_ref):
    x = x_ref[...].astype(F32)
    ms = jnp.mean(x * x, axis=-1, keepdims=True)
    o_ref[...] = ((x * lax.rsqrt(ms + EPS)) * g_ref[...]).astype(o_ref.dtype)


def rmsnorm(x, g, out_dtype):
    m, d = x.shape
    tm = _tile(m, 256)
    return pl.pallas_call(
        _rmsnorm_kernel,
        grid=(m // tm,),
        in_specs=[pl.BlockSpec((tm, d), lambda i: (i, 0)),
                  pl.BlockSpec((1, d), lambda i: (0, 0))],
        out_specs=pl.BlockSpec((tm, d), lambda i: (i, 0)),
        out_shape=jax.ShapeDtypeStruct((m, d), out_dtype),
        compiler_params=_cparams("parallel"),
        name="rmsnorm",
    )(x, g.reshape(1, d).astype(F32))


def _mm_kernel(a_ref, b_ref, *rest, nk, epilogue, cast, mean_dim):
    rest = list(rest)
    r_ref = rest.pop(0) if epilogue == "residual" else None
    ssq_ref = rest.pop(0) if mean_dim else None
    if cast:
        c_ref, o_ref, co_ref = rest
        co_ref[...] = c_ref[...].astype(co_ref.dtype)
    else:
        (o_ref,) = rest

    if nk == 1:
        acc = jnp.dot(a_ref[...], b_ref[...], preferred_element_type=F32)
        if mean_dim:
            ms = jnp.sum(ssq_ref[...], axis=-1, keepdims=True) * (1.0 / mean_dim)
            acc = acc * lax.rsqrt(ms + EPS)
        if epilogue == "relu2":
            acc = jnp.square(jnp.maximum(acc, 0.0))
        elif epilogue == "residual":
            acc = r_ref[...] + acc
        o_ref[...] = acc.astype(o_ref.dtype)
        return

    @pl.when(pl.program_id(2) == 0)
    def _():
        o_ref[...] = r_ref[...]

    o_ref[...] += jnp.dot(a_ref[...], b_ref[...], preferred_element_type=F32)


def matmul(a, w, *, out_dtype, epilogue="none", res=None, cast=None, row_ssq=None,
           tm=1024, tn=1024, tk=4096, name="matmul"):
    m, kdim = a.shape
    _, n = w.shape
    tm, tn, tk = _tile(m, tm), _tile(n, tn), _tile(kdim, tk)
    gm, gn, nk = m // tm, n // tn, kdim // tk
    in_specs = [pl.BlockSpec((tm, tk), lambda i, j, k: (i, k)),
                pl.BlockSpec((tk, tn), lambda i, j, k: (k, j))]
    args = [a, w]
    if epilogue == "residual":
        in_specs.append(pl.BlockSpec((tm, tn), lambda i, j, k: (i, j)))
        args.append(res)
    if row_ssq is not None:
        assert nk == 1
        in_specs.append(pl.BlockSpec((tm, LANES), lambda i, j, k: (i, 0)))
        args.append(row_ssq)
    out_specs = [pl.BlockSpec((tm, tn), lambda i, j, k: (i, j))]
    out_shape = [jax.ShapeDtypeStruct((m, n), out_dtype)]
    if cast is not None:
        w_src, layer = cast
        _, ck, cn = w_src.shape
        rows = ck // (gm * gn * nk)
        assert rows * gm * gn * nk == ck and rows % BF16_SUBLANES == 0
        step = lambda i, j, k: (i * gn + j) * nk + k
        in_specs.append(pl.BlockSpec((None, rows, cn), lambda i, j, k: (layer, step(i, j, k), 0)))
        args.append(w_src)
        out_specs.append(pl.BlockSpec((rows, cn), lambda i, j, k: (step(i, j, k), 0)))
        out_shape.append(jax.ShapeDtypeStruct((ck, cn), a.dtype))
    assert nk == 1 or (epilogue == "residual" and out_dtype == F32)
    outs = pl.pallas_call(
        functools.partial(_mm_kernel, nk=nk, epilogue=epilogue, cast=cast is not None,
                          mean_dim=kdim if row_ssq is not None else 0),
        grid=(gm, gn, nk),
        in_specs=in_specs,
        out_specs=out_specs,
        out_shape=out_shape,
        compiler_params=_cparams("parallel", "parallel", "arbitrary"),
        name=name,
    )(*args)
    return outs if cast is not None else outs[0]


def _outproj_kernel(a1_ref, a2_ref, b1_ref, b2_ref, r_ref, g_ref, o_ref, hg_ref, ssq_ref):
    acc = jnp.dot(a1_ref[...], b1_ref[...], preferred_element_type=F32)
    acc = acc + jnp.dot(a2_ref[...], b2_ref[...], preferred_element_type=F32)
    h = r_ref[...] + acc
    o_ref[...] = h
    hg_ref[...] = (h * g_ref[...]).astype(hg_ref.dtype)
    h2 = h * h
    part = h2[:, 0:LANES]
    for c in range(1, h2.shape[1] // LANES):
        part = part + h2[:, c * LANES:(c + 1) * LANES]

    @pl.when(pl.program_id(1) == 0)
    def _():
        ssq_ref[...] = part

    @pl.when(pl.program_id(1) > 0)
    def _():
        ssq_ref[...] += part


def outproj(a1, a2, w, res, gain, *, tm=1024, tn=512):
    m, k1 = a1.shape
    _, k2 = a2.shape
    _, n = w.shape
    assert k1 == k2
    tm, tn = _tile(m, tm), _tile(n, tn)
    return pl.pallas_call(
        _outproj_kernel,
        grid=(m // tm, n // tn),
        in_specs=[pl.BlockSpec((tm, k1), lambda i, j: (i, 0)),
                  pl.BlockSpec((tm, k2), lambda i, j: (i, 0)),
                  pl.BlockSpec((k1, tn), lambda i, j: (0, j)),
                  pl.BlockSpec((k2, tn), lambda i, j: (1, j)),
                  pl.BlockSpec((tm, tn), lambda i, j: (i, j)),
                  pl.BlockSpec((1, tn), lambda i, j: (0, j))],
        out_specs=[pl.BlockSpec((tm, tn), lambda i, j: (i, j)),
                   pl.BlockSpec((tm, tn), lambda i, j: (i, j)),
                   pl.BlockSpec((tm, LANES), lambda i, j: (i, 0))],
        out_shape=[jax.ShapeDtypeStruct((m, n), F32),
                   jax.ShapeDtypeStruct((m, n), a1.dtype),
                   jax.ShapeDtypeStruct((m, LANES), F32)],
        compiler_params=_cparams("parallel", "arbitrary"),
        name="outproj",
    )(a1, a2, w, w, res, gain.reshape(1, n).astype(F32))


def _glu_kernel(y_ref, w_ref, b_ref, *rest, cast):
    if cast:
        c_ref, o_ref, co_ref = rest
        co_ref[...] = c_ref[...].astype(co_ref.dtype)
    else:
        (o_ref,) = rest
    y = y_ref[...]
    z = jnp.dot(y, w_ref[...], preferred_element_type=F32) + b_ref[...]
    o_ref[...] = (y.astype(F32) * jax.nn.sigmoid(z)).astype(o_ref.dtype)


def glu(y, w, layer, b, *, cast=None, tm=512):
    m, n = y.shape
    tm = _tile(m, tm)
    steps = m // tm
    in_specs = [pl.BlockSpec((tm, n), lambda i: (i, 0)),
                pl.BlockSpec((None, n, n), lambda i: (layer, 0, 0)),
                pl.BlockSpec((1, n), lambda i: (0, 0))]
    args = [y, w, b.reshape(1, n).astype(F32)]
    out_specs = [pl.BlockSpec((tm, n), lambda i: (i, 0))]
    out_shape = [jax.ShapeDtypeStruct((m, n), BF16)]
    if cast is not None:
        w_src, src_layer = cast
        _, ck, cn = w_src.shape
        rows = ck // steps
        assert rows * steps == ck and rows % BF16_SUBLANES == 0
        in_specs.append(pl.BlockSpec((None, rows, cn), lambda i: (src_layer, i, 0)))
        args.append(w_src)
        out_specs.append(pl.BlockSpec((rows, cn), lambda i: (i, 0)))
        out_shape.append(jax.ShapeDtypeStruct((ck, cn), y.dtype))
    outs = pl.pallas_call(
        functools.partial(_glu_kernel, cast=cast is not None),
        grid=(steps,),
        in_specs=in_specs,
        out_specs=out_specs,
        out_shape=out_shape,
        compiler_params=_cparams("parallel"),
        name="glu",
    )(*args)
    return outs if cast is not None else outs[0]


def _t5_bucket_np(n, num_buckets):
    n = np.maximum(n, 0)
    nf = np.maximum(n, 1).astype(np.float32)
    large = MAX_EXACT + (np.log(nf / np.float32(MAX_EXACT)) / np.float32(math.log(MAX_DISTANCE / MAX_EXACT))
                         * np.float32(num_buckets - MAX_EXACT)).astype(np.int32)
    large = np.minimum(large, num_buckets - 1)
    return np.where(n < MAX_EXACT, n, large).astype(np.int32)


def _bias_tiles_kernel(rb_ref, idx_ref, o_ref, *, num_buckets):
    h = pl.program_id(0)
    for d in range(idx_ref.shape[0]):
        idx = idx_ref[d]
        acc = jnp.full(idx.shape, NEG_BIG, F32)
        for k in range(num_buckets):
            acc = jnp.where(idx == k, rb_ref[k, h] * LOG2E, acc)
        o_ref[0, d] = acc


def bias_tiles(rel_bias, t, seq):
    num_buckets, n_heads = rel_bias.shape
    c = np.arange(t)[:, None]
    r = np.arange(t)[None, :]
    far = _t5_bucket_np(np.arange(t + 1, max(seq, t + 2)), num_buckets)
    assert np.all(far == far[0]), "kv tiles two or more steps left of the diagonal must share one bucket"
    idx_far = np.full((t, t), far[0])
    idx_left = _t5_bucket_np(t + r - c, num_buckets)
    idx_diag = np.where(r - c >= 0, _t5_bucket_np(r - c, num_buckets), -1)
    idx = jnp.asarray(np.stack([idx_far, idx_left, idx_diag]).astype(np.int32))
    return pl.pallas_call(
        functools.partial(_bias_tiles_kernel, num_buckets=num_buckets),
        grid=(n_heads,),
        in_specs=[pl.BlockSpec(memory_space=pltpu.SMEM),
                  pl.BlockSpec((3, t, t), lambda h: (0, 0, 0))],
        out_specs=pl.BlockSpec((1, 3, t, t), lambda h: (h, 0, 0, 0)),
        out_shape=jax.ShapeDtypeStruct((n_heads, 3, t, t), F32),
        compiler_params=_cparams("arbitrary"),
        name="bias_tiles",
    )(rel_bias.astype(F32), idx)


def _attn_kernel(lamv_ref, gain_ref, bias_ref, q_ref, k_ref, v_ref, c_ref, o_ref, co_ref,
                 vt_ref, qt_ref, sa_ref, sb_ref, mxa_ref, mxb_ref, m_ref, l_ref, acc_ref, *, t, lam_init):
    qi = pl.program_id(2)
    d = DIFF_HEAD_DIM
    hd = 2 * d
    nh, nkv = vt_ref.shape[0], vt_ref.shape[1]
    heads = range(nh)
    co_ref[...] = c_ref[...].astype(co_ref.dtype)

    def lanes(e):
        return slice(e * hd, (e + 1) * hd)

    @pl.when(qi == 0)
    def _():
        for e in heads:
            for c in range(nkv):
                vt_ref[e, c] = v_ref[c * t:(c + 1) * t, lanes(e)].astype(F32).T.astype(vt_ref.dtype)

    row = lax.broadcasted_iota(jnp.int32, (hd, t), 0)
    for e in heads:
        qT = (q_ref[:, lanes(e)].astype(F32) * (d ** -0.5 * LOG2E)).T
        qt_ref[e, :, 0:t] = jnp.where(row < d, qT, 0.0).astype(qt_ref.dtype)
        qt_ref[e, :, t:2 * t] = jnp.where(row >= d, qT, 0.0).astype(qt_ref.dtype)

    m_ref[...] = jnp.full(m_ref.shape, NEG_BIG, F32)
    l_ref[...] = jnp.zeros(l_ref.shape, F32)
    acc_ref[...] = jnp.zeros(acc_ref.shape, F32)

    buf_a = (sa_ref, mxa_ref)
    buf_b = (sb_ref, mxb_ref)

    def scores(e, j, buf):
        s_ref, mx_ref = buf
        kblk = k_ref[pl.ds(pl.multiple_of(j * t, t), t), lanes(e)]
        s = jnp.dot(kblk, qt_ref[e], preferred_element_type=F32)
        b = bias_ref[e, jnp.clip(j - qi + 2, 0, 2)]
        for c in range(2):
            sc = s[:, c * t:(c + 1) * t] + b
            s_ref[e, :, c * t:(c + 1) * t] = sc
            mx_ref[e, :, c * t:(c + 1) * t] = jnp.max(sc, axis=0, keepdims=True)

    def reduce(e, buf):
        s_ref, mx_ref = buf
        m_prev = m_ref[e]
        m_new = jnp.maximum(m_prev, mx_ref[e])
        alpha = jnp.exp2(m_prev - m_new)
        p = jnp.exp2(s_ref[e] - m_new)
        l_ref[e] = alpha * l_ref[e] + jnp.sum(p, axis=0, keepdims=True)
        m_ref[e] = m_new
        return alpha, p.astype(vt_ref.dtype)

    def accumulate(e, j, alpha, p):
        acc_ref[e] = alpha * acc_ref[e] + jnp.dot(vt_ref[e, j], p, preferred_element_type=F32)

    def step(j, cur, nxt):
        ap = [reduce(e, cur) for e in heads]
        for e in heads:
            scores(e, j + 1, nxt)
        for e in heads:
            accumulate(e, j, *ap[e])

    odd = (qi & 1) == 1

    @pl.when(jnp.logical_not(odd))
    def _():
        for e in heads:
            scores(e, 0, buf_a)

    @pl.when(odd)
    def _():
        for e in heads:
            scores(e, 0, buf_b)
        step(0, buf_b, buf_a)

    def pair(pp, carry):
        j = (qi & 1) + 2 * pp
        step(j, buf_a, buf_b)
        step(j + 1, buf_b, buf_a)
        return carry

    lax.fori_loop(0, lax.shift_right_logical(qi, 1), pair, 0)
    ap = [reduce(e, buf_a) for e in heads]
    for e in heads:
        accumulate(e, qi, *ap[e])

    lv = lamv_ref[...]
    lam = (jnp.exp(jnp.sum(lv[0:1] * lv[1:2], axis=-1, keepdims=True))
           - jnp.exp(jnp.sum(lv[2:3] * lv[3:4], axis=-1, keepdims=True)) + lam_init)
    for e in heads:
        on = acc_ref[e] * (1.0 / l_ref[e])
        o = on[:, 0:t] - lam * on[:, t:2 * t]
        o = o * lax.rsqrt(jnp.mean(o * o, axis=0, keepdims=True) + EPS)
        o = (o * gain_ref[...]) * (1.0 - lam_init)
        o_ref[:, lanes(e)] = o.T.astype(o_ref.dtype)


def diff_attention(qkv, bias, rel_bias, lamv, gain, cast, *, batch, seq, lam_init, t):
    n_heads = rel_bias.shape[1]
    hd = 2 * DIFF_HEAD_DIM
    nq = seq // t
    nh = 2 if n_heads % 2 == 0 else 1
    ng = n_heads // nh
    w_src, layer = cast
    _, ck, cn = w_src.shape
    rows = ck // (batch * ng * nq)
    assert rows * batch * ng * nq == ck and rows % BF16_SUBLANES == 0
    step = lambda b, h, i: (b * ng + h) * nq + i
    kernel = functools.partial(_attn_kernel, t=t, lam_init=lam_init)
    stat = pltpu.VMEM((nh, 1, 2 * t), F32)
    tile = pltpu.VMEM((nh, t, 2 * t), F32)
    return pl.pallas_call(
        kernel,
        grid=(batch, ng, nq),
        in_specs=[pl.BlockSpec((4, DIFF_HEAD_DIM), lambda b, h, i: (0, 0)),
                  pl.BlockSpec((hd, 1), lambda b, h, i: (0, 0)),
                  pl.BlockSpec((nh, 3, t, t), lambda b, h, i: (h, 0, 0, 0)),
                  pl.BlockSpec((t, nh * hd), lambda b, h, i: (b * nq + i, h)),
                  pl.BlockSpec((seq, nh * hd), lambda b, h, i: (b, ng + h)),
                  pl.BlockSpec((seq, nh * hd), lambda b, h, i: (b, 2 * ng + h)),
                  pl.BlockSpec((None, rows, cn), lambda b, h, i: (layer, step(b, h, i), 0))],
        out_specs=[pl.BlockSpec((t, nh * hd), lambda b, h, i: (b * nq + i, h)),
                   pl.BlockSpec((rows, cn), lambda b, h, i: (step(b, h, i), 0))],
        out_shape=[jax.ShapeDtypeStruct((batch * seq, n_heads * hd), BF16),
                   jax.ShapeDtypeStruct((ck, cn), BF16)],
        scratch_shapes=[pltpu.VMEM((nh, nq, hd, t), BF16), pltpu.VMEM((nh, hd, 2 * t), BF16),
                        tile, tile, stat, stat, stat, stat,
                        pltpu.VMEM((nh, hd, 2 * t), F32)],
        compiler_params=_cparams("parallel", "parallel", "arbitrary"),
        name="diff_attention",
    )(lamv, gain.reshape(hd, 1).astype(F32), bias, qkv, qkv, qkv, w_src)


def _s5_prep_kernel(lre_ref, lim_ref, ldt_ref, btr_ref, bti_ref, cr_ref, ci_ref,
                    tg_ref, wg_ref, vgt_ref, ap_ref, es_ref, qs_ref, *, gb, chunk, nlev):
    L = chunk
    P = lre_ref.shape[-1]
    W = SSM_GROUP * L
    rows = es_ref.shape[2]
    lane = lax.broadcasted_iota(jnp.int32, (1, 2 * P), 1)
    sgn = jnp.where(lane < P, -1.0, 1.0).astype(F32)
    jj = lax.broadcasted_iota(jnp.int32, (rows, 2 * P), 0).astype(F32)

    def dup(x):
        return jnp.concatenate([x, x], axis=-1)

    def group(g, e_ref, q_ref):
        dt = jnp.exp(ldt_ref[g])
        lr, li = lre_ref[g], lim_ref[g]
        lrdt, lidt = lr * dt, li * dt
        mag = jnp.exp(lrdt)
        ar, ai = mag * jnp.cos(lidt), mag * jnp.sin(lidt)
        den = lr * lr + li * li
        nr, ni = ar - 1.0, ai
        gr = (nr * lr + ni * li) / den
        gi = (ni * lr - nr * li) / den
        btr, bti = btr_ref[g], bti_ref[g]
        bbr = gr * btr - gi * bti
        bbi = gr * bti + gi * btr
        cr, ci = cr_ref[g], ci_ref[g]
        bc1 = jnp.concatenate([bbr, bbi], axis=-1)
        bc2 = jnp.concatenate([-bbi, bbr], axis=-1)
        cc1 = jnp.concatenate([cr, -ci], axis=-1)
        cc2 = jnp.concatenate([-ci, -cr], axis=-1)

        mg = jnp.exp(jj * dup(lrdt))
        ang = jj * dup(lidt)
        e_ref[0] = mg * jnp.cos(ang)
        e_ref[1] = mg * jnp.sin(ang)

        for j in range(L + 1):
            er = jnp.broadcast_to(e_ref[0, j:j + 1, :], (SSM_GROUP, 2 * P))
            ei = jnp.broadcast_to(e_ref[1, j:j + 1, :], (SSM_GROUP, 2 * P))
            q_ref[j * SSM_GROUP:(j + 1) * SSM_GROUP, :] = er * cc1 + ei * cc2
            if j < L:
                tau = L - 1 - j
                wg_ref[g, tau * SSM_GROUP:(tau + 1) * SSM_GROUP, :] = (er * bc1 + ei * bc2).astype(wg_ref.dtype)

        vgt_ref[g] = q_ref[SSM_GROUP:SSM_GROUP * (L + 1), :].astype(vgt_ref.dtype)
        strip = lax.dot_general(bc1, q_ref[0:W, :], (((1,), (1,)), ((), ())),
                                precision=lax.Precision.HIGHEST, preferred_element_type=F32)
        pad = jnp.concatenate([jnp.zeros_like(strip), strip], axis=-1)
        lanes_per_step = LANES // SSM_GROUP
        for r in range(lanes_per_step):
            rolled = pad if r == 0 else pltpu.roll(pad, SSM_GROUP * r, axis=1)
            for qd in range(L // lanes_per_step):
                tau = lanes_per_step * qd + r
                tg_ref[g, tau * SSM_GROUP:(tau + 1) * SSM_GROUP, :] = (
                    rolled[:, W - LANES * qd:2 * W - LANES * qd].astype(tg_ref.dtype))

        pr, pi = e_ref[0, L:L + 1, :], e_ref[1, L:L + 1, :]
        for k in range(nlev):
            ap_ref[g, k, 0:1, :] = pr
            ap_ref[g, k, 1:2, :] = pi * sgn
            pr, pi = pr * pr - pi * pi, 2.0 * pr * pi

    def groups(gg, carry):
        for u in range(GROUPS_PER_ITER):
            group(gg * GROUPS_PER_ITER + u, es_ref.at[u], qs_ref.at[u])
        return carry

    lax.fori_loop(0, gb // GROUPS_PER_ITER, groups, 0)


def s5_prep(lam_re, lam_im, log_dt, bt_re, bt_im, c_re, c_im, *, chunk, nlev, gb=8):
    G, P = lam_re.shape
    H = SSM_GROUP
    W = H * chunk
    assert 2 * P == LANES and chunk % (LANES // H) == 0
    gb = _tile(G, gb)
    rows = -(-(chunk + 1) // 8) * 8
    vec = lambda a: a.reshape(G, 1, -1).astype(F32)
    g3 = lambda s1, s2: pl.BlockSpec((gb, s1, s2), lambda i: (i, 0, 0))
    return pl.pallas_call(
        functools.partial(_s5_prep_kernel, gb=gb, chunk=chunk, nlev=nlev),
        grid=(G // gb,),
        in_specs=[g3(1, P), g3(1, P), g3(1, 1), g3(H, P), g3(H, P), g3(H, P), g3(H, P)],
        out_specs=[g3(W, W), g3(W, 2 * P), g3(W, 2 * P),
                   pl.BlockSpec((gb, nlev, 2, 2 * P), lambda i: (i, 0, 0, 0))],
        out_shape=[jax.ShapeDtypeStruct((G, W, W), BF16),
                   jax.ShapeDtypeStruct((G, W, 2 * P), BF16),
                   jax.ShapeDtypeStruct((G, W, 2 * P), BF16),
                   jax.ShapeDtypeStruct((G, nlev, 2, 2 * P), F32)],
        scratch_shapes=[pltpu.VMEM((GROUPS_PER_ITER, 2, rows, 2 * P), F32),
                        pltpu.VMEM((GROUPS_PER_ITER, H * (chunk + 1), 2 * P), F32)],
        compiler_params=_cparams("parallel"),
        name="s5_prep",
    )(vec(lam_re), vec(lam_im), vec(log_dt), bt_re.astype(F32), bt_im.astype(F32),
      c_re.astype(F32), c_im.astype(F32))


def _s5_kernel(x_ref, perm_ref, tg_ref, wg_ref, vgt_ref, ap_ref, d_ref, o_ref, xs_ref, ys_ref,
               *, gb, nc, nlev, na):
    n = x_ref.shape[1]
    P2 = wg_ref.shape[-1]
    cidx = lax.broadcasted_iota(jnp.int32, (n, P2), 0) % nc

    def shift_rows(s, sh):
        return jnp.where(cidx >= sh, pltpu.roll(s, sh, axis=0), 0.0)

    for a in range(na):
        seg = jnp.concatenate([x_ref[OCTET * a + b] for b in range(OCTET)], axis=-1)
        seg = jnp.dot(seg, perm_ref[...], preferred_element_type=F32).astype(xs_ref.dtype)
        for g in range(gb):
            xs_ref[g, a] = seg[:, g * LANES:(g + 1) * LANES]

    def groups(gg, carry):
        gs = [gg * GROUPS_PER_ITER + u for u in range(GROUPS_PER_ITER)]
        xb = [jnp.concatenate([xs_ref[g, a] for a in range(na)], axis=-1) for g in gs]
        s = [jnp.dot(x, wg_ref[g], preferred_element_type=F32) for x, g in zip(xb, gs)]
        for k in range(nlev):
            sh = [shift_rows(v, 1 << k) for v in s]
            s = [v + w * ap_ref[g, k, 0:1, :] + pltpu.roll(w, P2 // 2, axis=1) * ap_ref[g, k, 1:2, :]
                 for v, w, g in zip(s, sh, gs)]
        for x, v, g in zip(xb, s, gs):
            s_in = shift_rows(v, 1)
            y = jnp.dot(x, tg_ref[g], preferred_element_type=F32)
            y = y + lax.dot_general(s_in.astype(BF16), vgt_ref[g], (((1,), (1,)), ((), ())),
                                    preferred_element_type=F32)
            y = y + x.astype(F32) * d_ref[g]
            yb = jax.nn.gelu(y).astype(ys_ref.dtype)
            for a in range(na):
                ys_ref[a, g] = yb[:, a * LANES:(a + 1) * LANES]
        return carry

    lax.fori_loop(0, gb // GROUPS_PER_ITER, groups, 0)

    for a in range(na):
        seg = jnp.concatenate([ys_ref[a, g] for g in range(gb)], axis=-1)
        seg = lax.dot_general(seg, perm_ref[...], (((1,), (1,)), ((), ())), preferred_element_type=F32)
        for b in range(OCTET):
            o_ref[OCTET * a + b] = seg[:, b * LANES:(b + 1) * LANES].astype(o_ref.dtype)


def s5_apply(ut, tg, wg, vgt, ap, dt, *, nc, nlev):
    L, n, width = ut.shape
    G = width // SSM_GROUP
    W = SSM_GROUP * L
    gb = LANES // SSM_GROUP
    assert G % gb == 0 and L % OCTET == 0 and gb * SSM_GROUP * OCTET == OCTET * LANES
    lane = np.arange(OCTET * LANES)
    b, g, h = lane // LANES, (lane % LANES) // SSM_GROUP, lane % SSM_GROUP
    perm = np.zeros((OCTET * LANES, OCTET * LANES), np.float32)
    perm[lane, g * (OCTET * SSM_GROUP) + b * SSM_GROUP + h] = 1.0
    g3 = lambda s1, s2: pl.BlockSpec((gb, s1, s2), lambda i: (i, 0, 0))
    io = pl.BlockSpec((L, n, LANES), lambda i: (0, 0, i))
    return pl.pallas_call(
        functools.partial(_s5_kernel, gb=gb, nc=nc, nlev=nlev, na=L // OCTET),
        grid=(G // gb,),
        in_specs=[io, pl.BlockSpec(perm.shape, lambda i: (0, 0)), g3(W, W), g3(W, LANES), g3(W, LANES),
                  pl.BlockSpec((gb, nlev, 2, LANES), lambda i: (i, 0, 0, 0)), g3(1, W)],
        out_specs=io,
        out_shape=jax.ShapeDtypeStruct((L, n, width), BF16),
        scratch_shapes=[pltpu.VMEM((gb, L // OCTET, n, LANES), BF16),
                        pltpu.VMEM((L // OCTET, gb, n, LANES), BF16)],
        compiler_params=_cparams("parallel"),
        name="s5_apply",
    )(ut, jnp.asarray(perm, BF16), tg, wg, vgt, ap, dt)


def s5_layer(u, lam_re, lam_im, log_dt, b_re, b_im, c_re, c_im, d, *, batch, seq):
    T, width = u.shape
    G = width // SSM_GROUP
    L = _tile(seq, SSM_CHUNK)
    nc = seq // L
    nlev = max(nc - 1, 0).bit_length()
    n = batch * nc
    W = SSM_GROUP * L
    bt_re = jnp.swapaxes(b_re, -1, -2)
    bt_im = jnp.swapaxes(b_im, -1, -2)
    tg, wg, vgt, ap = s5_prep(lam_re, lam_im, log_dt, bt_re, bt_im, c_re, c_im, chunk=L, nlev=max(nlev, 1))
    ut = u.reshape(n, L, width).swapaxes(0, 1)
    dtile = jnp.tile(d.astype(F32), (1, L)).reshape(G, 1, W)
    yt = s5_apply(ut, tg, wg, vgt, ap, dtile, nc=nc, nlev=nlev)
    return yt.swapaxes(0, 1).reshape(T, width)


@jax.jit
def _forward(x, w_in, w_out, norm_mix, norm_mlp, w_up, w_down, norm_final, rel_bias,
             lambda_q1, lambda_k1, lambda_q2, lambda_k2, subln_gain,
             ssm_lambda_re, ssm_lambda_im, ssm_log_dt, ssm_b_re, ssm_b_im,
             ssm_c_re, ssm_c_im, ssm_d, w_glu, b_glu):
    batch, seq, d_model = x.shape
    depth = w_in.shape[0]
    ssm_width = w_glu.shape[-1]
    attn_width = (w_in.shape[-1] - ssm_width) // 3
    T = batch * seq

    t_attn = _tile(seq, 512)
    bias = bias_tiles(rel_bias, t_attn, seq)

    w_glu_b, w_in_b = w_glu.astype(BF16), w_in[0].astype(BF16)
    h = x.reshape(T, d_model).astype(F32)
    for i in range(depth):
        lam_init = 0.8 - 0.6 * math.exp(-0.3 * i)
        xn = rmsnorm(h, norm_mix[i], BF16)
        proj, w_up_b = matmul(xn, w_in_b, out_dtype=BF16, cast=(w_up, i), name="proj_in")
        lamv = jnp.stack([lambda_q1[i], lambda_k1[i], lambda_q2[i], lambda_k2[i]]).astype(F32)
        attn, w_out_b = diff_attention(proj, bias, rel_bias, lamv, subln_gain[i], (w_out, i),
                                       batch=batch, seq=seq, lam_init=lam_init, t=t_attn)
        y = s5_layer(proj[:, 3 * attn_width:], ssm_lambda_re[i], ssm_lambda_im[i], ssm_log_dt[i],
                     ssm_b_re[i], ssm_b_im[i], ssm_c_re[i], ssm_c_im[i], ssm_d[i], batch=batch, seq=seq)
        if i + 1 < depth:
            ssm, w_in_b = glu(y, w_glu_b, i, b_glu[i], cast=(w_in, i + 1))
        else:
            ssm = glu(y, w_glu_b, i, b_glu[i])
        h, hg, ssq = outproj(attn, ssm, w_out_b, h, norm_mlp[i])
        ff, w_down_b = matmul(hg, w_up_b, out_dtype=BF16, epilogue="relu2", cast=(w_down, i), row_ssq=ssq,
                              name="mlp_up")
        h = matmul(ff, w_down_b, out_dtype=F32, epilogue="residual", res=h, tk=4096, name="mlp_down")
    out = rmsnorm(h, norm_final, x.dtype)
    return out.reshape(batch, seq, d_model)


def kernel(x, w_in, w_out, norm_mix, norm_mlp, w_up, w_down, norm_final, rel_bias, lambda_q1, lambda_k1, lambda_q2, lambda_k2, subln_gain, ssm_lambda_re, ssm_lambda_im, ssm_log_dt, ssm_b_re, ssm_b_im, ssm_c_re, ssm_c_im, ssm_d, w_glu, b_glu):
    return _forward(x, w_in, w_out, norm_mix, norm_mlp, w_up, w_down, norm_final, rel_bias,
                    lambda_q1, lambda_k1, lambda_q2, lambda_k2, subln_gain,
                    ssm_lambda_re, ssm_lambda_im, ssm_log_dt, ssm_b_re, ssm_b_im,
                    ssm_c_re, ssm_c_im, ssm_d, w_glu, b_glu)
```
